```python
import jax, jax.numpy as jnp
from jax import lax
import numpy as np

D_MODEL = 1024
BATCH = 1
SEQ = 16384
DEPTH = 4

CTX_LEN = 256
GRID_W = 64
N_MIXERS = 2
CONV_KERNEL = 31
CONV_PAD = CONV_KERNEL // 2
RET_HEADS = 4
RET_DK = D_MODEL // RET_HEADS
RET_DV = 2 * RET_DK
RET_QK_W = RET_HEADS * RET_DK
RET_V_W = RET_HEADS * RET_DV
RET_IN_W = 2 * RET_QK_W + 3 * RET_V_W
RET_CHUNK = 128
ROPE_BASE = 10000.0
D_FF = -(-8 * D_MODEL // (3 * 256)) * 256
NORM_EPS = 1e-6
LN_EPS = 1e-5

kernel_name = "hybrid_conformer_retention_dit_backbone"


def rms_norm(x, g):
    xf = x.astype(jnp.float32)
    y = xf * lax.rsqrt(jnp.mean(jnp.square(xf), axis=-1, keepdims=True) + NORM_EPS)
    return y.astype(x.dtype) * g


def layer_norm(x, g, b):
    xf = x.astype(jnp.float32)
    mu = jnp.mean(xf, axis=-1, keepdims=True)
    var = jnp.mean(jnp.square(xf - mu), axis=-1, keepdims=True)
    return ((xf - mu) * lax.rsqrt(var + LN_EPS)).astype(x.dtype) * g + b


def swiglu(h, w_in, w_out):
    gt, up = jnp.split(h @ w_in, 2, axis=-1)
    return (jax.nn.silu(gt) * up) @ w_out


def conv_module(h, pw1_w, pw1_b, dw_w, dw_b, ln_g, ln_b, pw2_w, pw2_b):
    a, gt = jnp.split(h @ pw1_w + pw1_b, 2, axis=-1)
    u = a * jax.nn.sigmoid(gt)
    u = lax.conv_general_dilated(
        u, dw_w[:, None, :].astype(u.dtype), window_strides=(1,),
        padding=[(CONV_PAD, CONV_PAD)], dimension_numbers=("NWC", "WIO", "NWC"),
        feature_group_count=D_MODEL) + dw_b
    u = jax.nn.silu(layer_norm(u, ln_g, ln_b))
    return u @ pw2_w + pw2_b


def axial_rope(n_rows):
    row = jnp.repeat(jnp.arange(n_rows, dtype=jnp.float32), GRID_W)
    col = jnp.tile(jnp.arange(GRID_W, dtype=jnp.float32), n_rows)
    quarter = RET_DK // 4
    inv = ROPE_BASE ** (-jnp.arange(quarter, dtype=jnp.float32) / quarter)
    ar = row[:, None] * inv
    ac = col[:, None] * inv
    ang = jnp.concatenate([ar, ar, ac, ac], axis=-1)
    return jnp.cos(ang), jnp.sin(ang)


def apply_rope(t, cos, sin):
    r1, r2, c1, c2 = jnp.split(t, 4, axis=-1)
    rot = jnp.concatenate([-r2, r1, -c2, c1], axis=-1)
    return t * cos[None, :, None, :].astype(t.dtype) + rot * sin[None, :, None, :].astype(t.dtype)


def retention_scan(q, k, v, gamma, state0):
    b, h, t, _ = q.shape
    n = t // RET_CHUNK

    def chunks(a):
        return jnp.moveaxis(a.astype(jnp.float32).reshape(b, h, n, RET_CHUNK, a.shape[-1]), 2, 0)

    log_g = jnp.log(gamma.astype(jnp.float32))
    idx = jnp.arange(RET_CHUNK, dtype=jnp.float32)
    rel = idx[:, None] - idx[None, :]
    decay_intra = jnp.where(rel >= 0, jnp.exp(jnp.maximum(rel, 0.0)[None] * log_g[:, None, None]), 0.0)
    q_decay = jnp.exp((idx + 1.0)[None, :] * log_g[:, None])
    k_decay = jnp.exp((RET_CHUNK - 1.0 - idx)[None, :] * log_g[:, None])
    chunk_decay = jnp.exp(RET_CHUNK * log_g)[:, None, None]

    def step(state, qkv):
        qc, kc, vc = qkv
        scores = jnp.einsum('bhid,bhjd->bhij', qc, kc) * decay_intra
        out = (jnp.einsum('bhij,bhjv->bhiv', scores, vc)
               + jnp.einsum('bhid,bhdv->bhiv', qc, state) * q_decay[None, :, :, None])
        state = state * chunk_decay + jnp.einsum('bhjd,bhjv->bhdv', kc * k_decay[None, :, :, None], vc)
        return state, out

    state, out = lax.scan(step, state0, (chunks(q), chunks(k), chunks(v)))
    out = jnp.moveaxis(out, 0, 2).reshape(b, h, t, -1)
    return out, state


def head_norm(y):
    y = y * lax.rsqrt(jnp.mean(jnp.square(y), axis=-1, keepdims=True) + NORM_EPS)
    b, h, t, dv = y.shape
    return y.transpose(0, 2, 1, 3).reshape(b, t, h * dv)


def retention_mixer(h_lat, h_ctx, w_in, log2_eps, w_out, cos, sin, need_ctx_out):
    gamma = 1.0 - jnp.exp2(log2_eps.astype(jnp.float32))

    def project(h, rope):
        b, t, _ = h.shape
        q, k, v, gf, gb = jnp.split(
            h @ w_in, [RET_QK_W, 2 * RET_QK_W, 2 * RET_QK_W + RET_V_W, 2 * RET_QK_W + 2 * RET_V_W], axis=-1)
        q = q.reshape(b, t, RET_HEADS, RET_DK)
        k = k.reshape(b, t, RET_HEADS, RET_DK)
        if rope:
            q = apply_rope(q, cos, sin)
            k = apply_rope(k, cos, sin)
        k = k * (RET_DK ** -0.5)
        v = v.reshape(b, t, RET_HEADS, RET_DV)
        heads = lambda a: a.transpose(0, 2, 1, 3)
        return heads(q), heads(k), heads(v), gf, gb

    flip = lambda a: jnp.flip(a, axis=2)
    q_c, k_c, v_c, gf_c, gb_c = project(h_ctx, False)
    q_l, k_l, v_l, gf_l, gb_l = project(h_lat, True)
    zero = jnp.zeros((h_lat.shape[0], RET_HEADS, RET_DK, RET_DV), jnp.float32)

    yf_c, s_f = retention_scan(q_c, k_c, v_c, gamma[0], zero)
    yb_c, s_b = retention_scan(flip(q_c), flip(k_c), flip(v_c), gamma[1], zero)
    yf_l, _ = retention_scan(q_l, k_l, v_l, gamma[0], s_f)
    yb_l, _ = retention_scan(flip(q_l), flip(k_l), flip(v_l), gamma[1], s_b)

    def merge(yf, yb, gf, gb):
        y = jax.nn.silu(gf) * head_norm(yf).astype(gf.dtype) + jax.nn.silu(gb) * head_norm(yb).astype(gb.dtype)
        return y @ w_out

    out_lat = merge(yf_l, flip(yb_l), gf_l, gb_l)
    out_ctx = merge(yf_c, flip(yb_c), gf_c, gb_c) if need_ctx_out else None
    return out_lat, out_ctx


def setup_inputs(seed: int = 0) -> dict:
    key = jax.random.key(seed)
    ks = jax.random.split(key, 24)
    n_conv = (DEPTH + N_MIXERS - 1) // N_MIXERS
    n_ret = DEPTH // N_MIXERS
    nrm = lambda k, shape, fan_in, s=1.0: jax.random.normal(k, shape, jnp.float32) * (s * fan_in ** -0.5)
    gain = lambda k, shape: 1.0 + 0.02 * jax.random.normal(k, shape, jnp.float32)
    bias = lambda k, shape: 0.02 * jax.random.normal(k, shape, jnp.float32)
    log2_eps = (-5.0 - jnp.arange(RET_HEADS, dtype=jnp.float32))[None, None, :] \
        + 0.1 * jax.random.normal(ks[17], (n_ret, 2, RET_HEADS), jnp.float32)
    return {
        "x": jax.random.normal(ks[0], (BATCH, SEQ, D_MODEL), jnp.float32),
        "c": jax.random.normal(ks[1], (BATCH, D_MODEL), jnp.float32),
        "ctx": jax.random.normal(ks[2], (BATCH, CTX_LEN, D_MODEL), jnp.float32),
        "c_ctx": jax.random.normal(ks[3], (D_MODEL,), jnp.float32),
        "mod_w": nrm(ks[4], (DEPTH, D_MODEL, 6 * D_MODEL), D_MODEL, 0.5),
        "mod_b": bias(ks[5], (DEPTH, 6 * D_MODEL)),
        "norm1_g": gain(ks[6], (DEPTH, D_MODEL)),
        "norm2_g": gain(ks[7], (DEPTH, D_MODEL)),
        "conv_pw1_w": nrm(ks[8], (n_conv, D_MODEL, 2 * D_MODEL), D_MODEL),
        "conv_pw1_b": bias(ks[9], (n_conv, 2 * D_MODEL)),
        "conv_dw_w": nrm(ks[10], (n_conv, CONV_KERNEL, D_MODEL), CONV_KERNEL),
        "conv_dw_b": bias(ks[11], (n_conv, D_MODEL)),
        "conv_ln_g": gain(ks[12], (n_conv, D_MODEL)),
        "conv_ln_b": bias(ks[13], (n_conv, D_MODEL)),
        "conv_pw2_w": nrm(ks[14], (n_conv, D_MODEL, D_MODEL), D_MODEL),
        "conv_pw2_b": bias(ks[15], (n_conv, D_MODEL)),
        "ret_w_in": nrm(ks[16], (n_ret, D_MODEL, RET_IN_W), D_MODEL),
        "ret_log2_eps": log2_eps,
        "ret_w_out": nrm(ks[18], (n_ret, RET_V_W, D_MODEL), RET_V_W),
        "ffn_w_in": nrm(ks[19], (DEPTH, D_MODEL, 2 * D_FF), D_MODEL),
        "ffn_w_out": nrm(ks[20], (DEPTH, D_FF, D_MODEL), D_FF),
        "final_norm_g": gain(ks[21], (D_MODEL,)),
    }


def reference(x, c, ctx, c_ctx, mod_w, mod_b, norm1_g, norm2_g,
              conv_pw1_w, conv_pw1_b, conv_dw_w, conv_dw_b, conv_ln_g, conv_ln_b, conv_pw2_w, conv_pw2_b,
              ret_w_in, ret_log2_eps, ret_w_out, ffn_w_in, ffn_w_out, final_norm_g):
    n_tok = x.shape[1]
    ROWS = n_tok // GRID_W
    cos, sin = axial_rope(ROWS)
    silu_c = jax.nn.silu(c)[:, None, :]
    silu_cc = jax.nn.silu(c_ctx)
    ctx_s = ctx

    for i in range(DEPTH):
        last = i == DEPTH - 1
        j = i // N_MIXERS
        m_lat = jnp.split(silu_c @ mod_w[i] + mod_b[i], 6, axis=-1)
        m_ctx = jnp.split(silu_cc @ mod_w[i] + mod_b[i], 6, axis=-1)
        h_lat = rms_norm(x, norm1_g[i]) * (1 + m_lat[1]) + m_lat[0]

        if i % N_MIXERS == 0:
            conv_args = (conv_pw1_w[j], conv_pw1_b[j], conv_dw_w[j], conv_dw_b[j],
                         conv_ln_g[j], conv_ln_b[j], conv_pw2_w[j], conv_pw2_b[j])
            y_lat = conv_module(h_lat, *conv_args)
            if not last:
                h_ctx = rms_norm(ctx_s, norm1_g[i]) * (1 + m_ctx[1]) + m_ctx[0]
                y_ctx = conv_module(h_ctx, *conv_args)
        else:
            h_ctx = rms_norm(ctx_s, norm1_g[i]) * (1 + m_ctx[1]) + m_ctx[0]
            y_lat, y_ctx = retention_mixer(h_lat, h_ctx, ret_w_in[j], ret_log2_eps[j], ret_w_out[j],
                                           cos, sin, not last)

        x = x + m_lat[2] * y_lat
        x = x + m_lat[5] * swiglu(rms_norm(x, norm2_g[i]) * (1 + m_lat[4]) + m_lat[3], ffn_w_in[i], ffn_w_out[i])
        if not last:
            ctx_s = ctx_s + m_ctx[2] * y_ctx
            ctx_s = ctx_s + m_ctx[5] * swiglu(rms_norm(ctx_s, norm2_g[i]) * (1 + m_ctx[4]) + m_ctx[3],
                                              ffn_w_in[i], ffn_w_out[i])

    return rms_norm(x, final_norm_g)
```

```python
import functools

import jax
import jax.numpy as jnp
import numpy as np
from jax import lax
from jax.experimental import pallas as pl
from jax.experimental.pallas import tpu as pltpu

D_MODEL = 1024
DEPTH = 4
GRID_W = 64
N_MIXERS = 2
CONV_KERNEL = 31
CONV_PAD = CONV_KERNEL // 2
RET_HEADS = 4
RET_DK = D_MODEL // RET_HEADS
RET_DV = 2 * RET_DK
RET_QK_W = RET_HEADS * RET_DK
RET_V_W = RET_HEADS * RET_DV
RET_CHUNK = 128
ROPE_BASE = 10000.0
D_FF = 2816
NORM_EPS = 1e-6
LN_EPS = 1e-5

LANES = 128
HALO = 16
VMEM_LIMIT = 56 * 1024 * 1024

_BF16 = jnp.bfloat16
_F32 = jnp.float32


def _const_spec(shape):
    nd = len(shape)
    return pl.BlockSpec(shape, lambda i: (0,) * nd, pipeline_mode=pl.Buffered(1))


def _params(n_grid_axes=1):
    return pltpu.CompilerParams(
        dimension_semantics=("arbitrary",) * n_grid_axes,
        vmem_limit_bytes=VMEM_LIMIT)


def _modnorm(x, g, shift, scale):
    ms = jnp.mean(x * x, axis=-1, keepdims=True)
    return (x * lax.rsqrt(ms + NORM_EPS)) * (g * (1.0 + scale)) + shift


def _silu(v):
    return v * jax.nn.sigmoid(v)


def _dot(a, b):
    return jnp.dot(a, b, preferred_element_type=_F32)


MOD_TN = 1536


def _mod_kernel(cb_ref, w_ref, b_ref, o_ref):
    s_lat = _silu(cb_ref[0])
    s_ctx = _silu(cb_ref[1])
    o_ref[...] = jnp.zeros(o_ref.shape, _F32)
    for j in range(MOD_TN // LANES):
        cols = slice(j * LANES, (j + 1) * LANES)
        w = w_ref[0, :, cols]
        bias = b_ref[0, :, cols]
        o_ref[0, 0:1, cols] = jnp.sum(w * s_lat, axis=0, keepdims=True) + bias
        o_ref[0, 1:2, cols] = jnp.sum(w * s_ctx, axis=0, keepdims=True) + bias


def _modulation(c, c_ctx, mod_w, mod_b):
    cvec = jnp.stack([c[0], c_ctx])
    cb = jnp.broadcast_to(cvec[:, :, None], (2, D_MODEL, LANES))
    n_out = 6 * D_MODEL
    return pl.pallas_call(
        _mod_kernel,
        out_shape=jax.ShapeDtypeStruct((DEPTH, 8, n_out), _F32),
        grid=(DEPTH, n_out // MOD_TN),
        in_specs=[
            pl.BlockSpec((2, D_MODEL, LANES), lambda i, j: (0, 0, 0)),
            pl.BlockSpec((1, D_MODEL, MOD_TN), lambda i, j: (i, 0, j)),
            pl.BlockSpec((1, 1, MOD_TN), lambda i, j: (i, 0, j)),
        ],
        out_specs=pl.BlockSpec((1, 8, MOD_TN), lambda i, j: (i, 0, j)),
        compiler_params=_params(2),
        name="modulation",
    )(cb, mod_w, mod_b.reshape(DEPTH, 1, n_out))


def _ffn_kernel(x_ref, m_ref, g_ref, win_ref, wout_ref, fg_ref, o_ref, *, final):
    x = x_ref[...]
    h = _modnorm(x, g_ref[...], m_ref[3:4, :], m_ref[4:5, :]).astype(_BF16)
    gt = _dot(h, win_ref[:, :D_FF])
    up = _dot(h, win_ref[:, D_FF:])
    a = (_silu(gt) * up).astype(_BF16)
    y = x + m_ref[5:6, :] * _dot(a, wout_ref[...])
    if final:
        ms = jnp.mean(y * y, axis=-1, keepdims=True)
        y = (y * lax.rsqrt(ms + NORM_EPS)) * fg_ref[...]
    o_ref[...] = y


def _ffn(x, m, g, w_in, w_out, fg, *, final, tm):
    t = x.shape[0]
    row = pl.BlockSpec((tm, D_MODEL), lambda i: (i, 0))
    return pl.pallas_call(
        functools.partial(_ffn_kernel, final=final),
        out_shape=jax.ShapeDtypeStruct((t, D_MODEL), _F32),
        grid=(t // tm,),
        in_specs=[row, _const_spec((6, D_MODEL)), _const_spec((1, D_MODEL)),
                  _const_spec((D_MODEL, 2 * D_FF)), _const_spec((D_FF, D_MODEL)),
                  _const_spec((1, D_MODEL))],
        out_specs=row,
        compiler_params=_params(),
        name="ffn",
    )(x, m, g, w_in, w_out, fg)


def _conv_kernel(x_ref, xp_ref, xn_ref, m_ref, g_ref, pw1_ref, b1_ref, dw_ref, dwb_ref,
                 lng_ref, lnb_ref, pw2_ref, b2_ref, o_ref, u_scr, *, tm, t_total):
    i = pl.program_id(0)
    x = x_ref[...]
    xa = jnp.concatenate([xp_ref[...], x, xn_ref[...]], axis=0)
    h = _modnorm(xa, g_ref[...], m_ref[0:1, :], m_ref[1:2, :]).astype(_BF16)
    z = _dot(h, pw1_ref[...]) + b1_ref[...]
    u = z[:, :D_MODEL] * jax.nn.sigmoid(z[:, D_MODEL:])
    r = i * tm - HALO + lax.broadcasted_iota(jnp.int32, (tm + 2 * HALO, 1), 0)
    u_scr[...] = jnp.where((r >= 0) & (r < t_total), u, 0.0)
    acc = jnp.broadcast_to(dwb_ref[...], (tm, D_MODEL))
    for k in range(CONV_KERNEL):
        off = HALO - CONV_PAD + k
        acc = acc + u_scr[off:off + tm, :] * dw_ref[k:k + 1, :]
    mu = jnp.mean(acc, axis=-1, keepdims=True)
    d = acc - mu
    var = jnp.mean(d * d, axis=-1, keepdims=True)
    v = _silu(d * lax.rsqrt(var + LN_EPS) * lng_ref[...] + lnb_ref[...]).astype(_BF16)
    y = _dot(v, pw2_ref[...]) + b2_ref[...]
    o_ref[...] = x + m_ref[2:3, :] * y


def _conv_mixer(x, m, g, pw1, b1, dw, dwb, lng, lnb, pw2, b2, *, tm):
    t = x.shape[0]
    hb = tm // HALO
    n_hb = t // HALO
    row = pl.BlockSpec((tm, D_MODEL), lambda i: (i, 0))
    prev = pl.BlockSpec((HALO, D_MODEL), lambda i: (jnp.maximum(i * hb - 1, 0), 0))
    nxt = pl.BlockSpec((HALO, D_MODEL), lambda i: (jnp.minimum((i + 1) * hb, n_hb - 1), 0))
    vec = _const_spec((1, D_MODEL))
    return pl.pallas_call(
        functools.partial(_conv_kernel, tm=tm, t_total=t),
        out_shape=jax.ShapeDtypeStruct((t, D_MODEL), _F32),
        grid=(t // tm,),
        in_specs=[row, prev, nxt, _const_spec((6, D_MODEL)), vec,
                  _const_spec((D_MODEL, 2 * D_MODEL)), _const_spec((1, 2 * D_MODEL)),
                  _const_spec((CONV_KERNEL, D_MODEL)), vec, vec, vec,
                  _const_spec((D_MODEL, D_MODEL)), vec],
        out_specs=row,
        scratch_shapes=[pltpu.VMEM((tm + 2 * HALO, D_MODEL), _F32)],
        compiler_params=_params(),
        name="conv_mixer",
    )(x, x, x, m, g, pw1, b1, dw, dwb, lng, lnb, pw2, b2)


def _decay_tables(le_ref, dmat_scr, qd_scr, kd_scr, cd_scr, *, reverse):
    c = RET_CHUNK
    ri = lax.broadcasted_iota(jnp.int32, (c, c), 0)
    ci = lax.broadcasted_iota(jnp.int32, (c, c), 1)
    rel = (ci - ri) if reverse else (ri - ci)
    relf = jnp.maximum(rel, 0).astype(_F32)
    pos = lax.broadcasted_iota(jnp.int32, (c, RET_DV), 0).astype(_F32)
    posk = lax.broadcasted_iota(jnp.int32, (c, RET_DK), 0).astype(_F32)
    for hd in range(RET_HEADS):
        log_g = jnp.log(1.0 - jnp.exp2(le_ref[hd]))
        log_gk = jnp.log(1.0 - jnp.exp2(le_ref[hd, :, :RET_DK]))
        log_gc = jnp.log(1.0 - jnp.exp2(le_ref[hd, :, :c]))
        dmat_scr[hd] = jnp.where(rel >= 0, jnp.exp(relf * log_gc), 0.0)
        if reverse:
            qd_scr[hd] = jnp.exp((c - pos) * log_g)
            kd_scr[hd] = jnp.exp(posk * log_gk)
        else:
            qd_scr[hd] = jnp.exp((pos + 1.0) * log_g)
            kd_scr[hd] = jnp.exp((c - 1.0 - posk) * log_gk)
        cd_scr[hd] = jnp.exp(float(c) * log_g)


def _retention_chunk(q, k, v, hd, state_ref, dmat_scr, qd_scr, kd_scr, cd_scr):
    s = lax.dot_general(q, k, (((1,), (1,)), ((), ())), preferred_element_type=_F32)
    s = (s * dmat_scr[hd]).astype(_BF16)
    st = state_ref[hd]
    o = _dot(s, v) + _dot(q, st.astype(_BF16)) * qd_scr[hd]
    kd = (k.astype(_F32) * kd_scr[hd]).astype(_BF16)
    upd = lax.dot_general(kd, v, (((0,), (0,)), ((), ())), preferred_element_type=_F32)
    state_ref[hd] = st * cd_scr[hd] + upd
    return o


def _head_norm(o):
    return o * lax.rsqrt(jnp.mean(o * o, axis=-1, keepdims=True) + NORM_EPS)


def _ret_fwd_kernel(x_ref, cos_ref, sin_ref, m_ref, g_ref, w_ref, le_ref, s0_ref,
                    q_ref, k_ref, v_ref, y_ref, state_ref,
                    dmat_scr, qd_scr, kd_scr, cd_scr, gate_scr, *, tm, rope):
    @pl.when(pl.program_id(0) == 0)
    def _():
        _decay_tables(le_ref, dmat_scr, qd_scr, kd_scr, cd_scr, reverse=False)
        state_ref[...] = s0_ref[...]

    h = _modnorm(x_ref[...], g_ref[...], m_ref[0:1, :], m_ref[1:2, :]).astype(_BF16)
    qf = _dot(h, w_ref[:, :RET_QK_W])
    kf = _dot(h, w_ref[:, RET_QK_W:2 * RET_QK_W])
    k_scale = RET_DK ** -0.5
    for l in range(RET_QK_W // LANES):
        cols = slice(l * LANES, (l + 1) * LANES)
        qt = qf[:, cols]
        kt = kf[:, cols]
        if rope:
            tab = slice((l % 2) * LANES, (l % 2 + 1) * LANES)
            cs = cos_ref[:, tab]
            sn = sin_ref[:, tab]
            qt = qt * cs + pltpu.roll(qt, LANES // 2, axis=1) * sn
            kt = kt * cs + pltpu.roll(kt, LANES // 2, axis=1) * sn
        q_ref[:, cols] = qt.astype(_BF16)
        k_ref[:, cols] = (kt * k_scale).astype(_BF16)
    v_ref[...] = _dot(h, w_ref[:, 2 * RET_QK_W:2 * RET_QK_W + RET_V_W]).astype(_BF16)
    gate_scr[...] = _silu(_dot(h, w_ref[:, 2 * RET_QK_W + RET_V_W:]))

    for j in range(tm // RET_CHUNK):
        rows = slice(j * RET_CHUNK, (j + 1) * RET_CHUNK)
        for hd in range(RET_HEADS):
            qk_cols = slice(hd * RET_DK, (hd + 1) * RET_DK)
            v_cols = slice(hd * RET_DV, (hd + 1) * RET_DV)
            o = _retention_chunk(q_ref[rows, qk_cols], k_ref[rows, qk_cols], v_ref[rows, v_cols],
                                 hd, state_ref, dmat_scr, qd_scr, kd_scr, cd_scr)
            y_ref[rows, v_cols] = (gate_scr[rows, v_cols] * _head_norm(o)).astype(_BF16)


def _ret_bwd_kernel(x_ref, q_ref, k_ref, v_ref, yf_ref, m_ref, g_ref, wg_ref, wo_ref, le_ref,
                    s0_ref, o_ref, state_ref,
                    dmat_scr, qd_scr, kd_scr, cd_scr, gate_scr, y_scr, *, tm):
    @pl.when(pl.program_id(0) == 0)
    def _():
        _decay_tables(le_ref, dmat_scr, qd_scr, kd_scr, cd_scr, reverse=True)
        state_ref[...] = s0_ref[...]

    x = x_ref[...]
    h = _modnorm(x, g_ref[...], m_ref[0:1, :], m_ref[1:2, :]).astype(_BF16)
    gate_scr[...] = _silu(_dot(h, wg_ref[...]))

    for j in reversed(range(tm // RET_CHUNK)):
        rows = slice(j * RET_CHUNK, (j + 1) * RET_CHUNK)
        for hd in range(RET_HEADS):
            qk_cols = slice(hd * RET_DK, (hd + 1) * RET_DK)
            v_cols = slice(hd * RET_DV, (hd + 1) * RET_DV)
            o = _retention_chunk(q_ref[rows, qk_cols], k_ref[rows, qk_cols], v_ref[rows, v_cols],
                                 hd, state_ref, dmat_scr, qd_scr, kd_scr, cd_scr)
            y = yf_ref[rows, v_cols].astype(_F32) + gate_scr[rows, v_cols] * _head_norm(o)
            y_scr[rows, v_cols] = y.astype(_BF16)
    o_ref[...] = x + m_ref[2:3, :] * _dot(y_scr[...], wo_ref[...])


def _decay_scratch():
    return [pltpu.VMEM((RET_HEADS, RET_CHUNK, RET_CHUNK), _F32),
            pltpu.VMEM((RET_HEADS, RET_CHUNK, RET_DV), _F32),
            pltpu.VMEM((RET_HEADS, RET_CHUNK, RET_DK), _F32),
            pltpu.VMEM((RET_HEADS, 1, RET_DV), _F32)]


def _ret_fwd(x, cos, sin, m, g, w_fwd, le, s0, *, tm, rope):
    t = x.shape[0]
    row = lambda w: pl.BlockSpec((tm, w), lambda i: (i, 0))
    state_shape = (RET_HEADS, RET_DK, RET_DV)
    return pl.pallas_call(
        functools.partial(_ret_fwd_kernel, tm=tm, rope=rope),
        out_shape=(jax.ShapeDtypeStruct((t, RET_QK_W), _BF16),
                   jax.ShapeDtypeStruct((t, RET_QK_W), _BF16),
                   jax.ShapeDtypeStruct((t, RET_V_W), _BF16),
                   jax.ShapeDtypeStruct((t, RET_V_W), _BF16),
                   jax.ShapeDtypeStruct(state_shape, _F32)),
        grid=(t // tm,),
        in_specs=[row(D_MODEL), row(RET_DK), row(RET_DK), _const_spec((6, D_MODEL)),
                  _const_spec((1, D_MODEL)), _const_spec(w_fwd.shape),
                  _const_spec((RET_HEADS, 1, RET_DV)), _const_spec(state_shape)],
        out_specs=(row(RET_QK_W), row(RET_QK_W), row(RET_V_W), row(RET_V_W),
                   pl.BlockSpec(state_shape, lambda i: (0, 0, 0))),
        scratch_shapes=_decay_scratch() + [pltpu.VMEM((tm, RET_V_W), _F32)],
        compiler_params=_params(),
        name="retention_fwd",
    )(x, cos, sin, m, g, w_fwd, le, s0)


def _ret_bwd(x, q, k, v, yf, m, g, w_gb, w_out, le, s0, *, tm):
    t = x.shape[0]
    n = t // tm
    row = lambda w: pl.BlockSpec((tm, w), lambda i: (n - 1 - i, 0))
    state_shape = (RET_HEADS, RET_DK, RET_DV)
    return pl.pallas_call(
        functools.partial(_ret_bwd_kernel, tm=tm),
        out_shape=(jax.ShapeDtypeStruct((t, D_MODEL), _F32),
                   jax.ShapeDtypeStruct(state_shape, _F32)),
        grid=(n,),
        in_specs=[row(D_MODEL), row(RET_QK_W), row(RET_QK_W), row(RET_V_W), row(RET_V_W),
                  _const_spec((6, D_MODEL)), _const_spec((1, D_MODEL)),
                  _const_spec(w_gb.shape), _const_spec(w_out.shape),
                  _const_spec((RET_HEADS, 1, RET_DV)), _const_spec(state_shape)],
        out_specs=(row(D_MODEL), pl.BlockSpec(state_shape, lambda i: (0, 0, 0))),
        scratch_shapes=_decay_scratch() + [pltpu.VMEM((tm, RET_V_W), _F32),
                                           pltpu.VMEM((tm, RET_V_W), _BF16)],
        compiler_params=_params(),
        name="retention_bwd",
    )(x, q, k, v, yf, m, g, w_gb, w_out, le, s0)


def _rope_tables(n_tok):
    tok = np.arange(n_tok)
    row = (tok // GRID_W).astype(np.float32)
    col = (tok % GRID_W).astype(np.float32)
    quarter = RET_DK // 4
    inv = (ROPE_BASE ** (-np.arange(quarter, dtype=np.float32) / quarter)).astype(np.float32)
    ar = jnp.asarray(row[:, None] * inv)
    ac = jnp.asarray(col[:, None] * inv)
    cos = jnp.concatenate([jnp.cos(ar), jnp.cos(ar), jnp.cos(ac), jnp.cos(ac)], axis=-1)
    sin = jnp.concatenate([-jnp.sin(ar), jnp.sin(ar), -jnp.sin(ac), jnp.sin(ac)], axis=-1)
    return cos, sin


def kernel(x, c, ctx, c_ctx, mod_w, mod_b, norm1_g, norm2_g, conv_pw1_w, conv_pw1_b, conv_dw_w,
           conv_dw_b, conv_ln_g, conv_ln_b, conv_pw2_w, conv_pw2_b, ret_w_in, ret_log2_eps,
           ret_w_out, ffn_w_in, ffn_w_out, final_norm_g):
    assert x.shape[0] == 1 and x.shape[2] == D_MODEL
    n_tok = x.shape[1]
    n_ctx = ctx.shape[1]
    tm = 512
    assert n_tok % tm == 0 and n_ctx % RET_CHUNK == 0
    xs = x[0]
    cs = ctx[0]
    vec = lambda a: a.reshape(1, -1)

    mods = _modulation(c, c_ctx, mod_w, mod_b)
    cos, sin = _rope_tables(n_tok)
    zero_state = jnp.zeros((RET_HEADS, RET_DK, RET_DV), _F32)
    fg = vec(final_norm_g)

    for i in range(DEPTH):
        last = i == DEPTH - 1
        j = i // N_MIXERS
        m_lat = mods[i, 0].reshape(6, D_MODEL)
        m_ctx = mods[i, 1].reshape(6, D_MODEL)
        g1 = vec(norm1_g[i])
        g2 = vec(norm2_g[i])
        w_in = ffn_w_in[i].astype(_BF16)
        w_out = ffn_w_out[i].astype(_BF16)

        if i % N_MIXERS == 0:
            conv_args = (conv_pw1_w[j].astype(_BF16), vec(conv_pw1_b[j]), conv_dw_w[j],
                         vec(conv_dw_b[j]), vec(conv_ln_g[j]), vec(conv_ln_b[j]),
                         conv_pw2_w[j].astype(_BF16), vec(conv_pw2_b[j]))
            xs = _conv_mixer(xs, m_lat, g1, *conv_args, tm=tm)
            if not last:
                cs = _conv_mixer(cs, m_ctx, g1, *conv_args, tm=n_ctx)
        else:
            split = 2 * RET_QK_W + 2 * RET_V_W
            w_fwd = ret_w_in[j][:, :split].astype(_BF16)
            w_gb = ret_w_in[j][:, split:].astype(_BF16)
            w_o = ret_w_out[j].astype(_BF16)
            le = jnp.broadcast_to(ret_log2_eps[j][:, :, None, None], (2, RET_HEADS, 1, RET_DV))
            cq, ck, cv, cy, s_f = _ret_fwd(cs, cos[:n_ctx], sin[:n_ctx], m_ctx, g1, w_fwd, le[0],
                                           zero_state, tm=n_ctx, rope=False)
            cs_new, s_b = _ret_bwd(cs, cq, ck, cv, cy, m_ctx, g1, w_gb, w_o, le[1], zero_state,
                                   tm=n_ctx)
            lq, lk, lv, ly, _ = _ret_fwd(xs, cos, sin, m_lat, g1, w_fwd, le[0], s_f, tm=tm,
                                         rope=True)
            xs, _ = _ret_bwd(xs, lq, lk, lv, ly, m_lat, g1, w_gb, w_o, le[1], s_b, tm=tm)
            if not last:
                cs = cs_new

        xs = _ffn(xs, m_lat, g2, w_in, w_out, fg, final=last, tm=tm)
        if not last:
            cs = _ffn(cs, m_ctx, g2, w_in, w_out, fg, final=False, tm=n_ctx)

    return xs[None]
```

```python
import functools

import jax
import jax.numpy as jnp
from jax import lax
from jax.experimental import pallas as pl
from jax.experimental.pallas import tpu as pltpu

D_MODEL = 1024
DEPTH = 4
GRID_W = 64
N_MIXERS = 2
CONV_KERNEL = 31
CONV_PAD = CONV_KERNEL // 2
RET_HEADS = 4
RET_DK = D_MODEL // RET_HEADS
RET_DV = 2 * RET_DK
RET_QK_W = RET_HEADS * RET_DK
RET_V_W = RET_HEADS * RET_DV
RET_CHUNK = 256
ROPE_BASE = 10000.0
D_FF = 2816
NORM_EPS = 1e-6
LN_EPS = 1e-5

LANES = 128
HALO = 16
CONV_RB = 64
VMEM_LIMIT = 56 * 1024 * 1024

_BF16 = jnp.bfloat16
_F32 = jnp.float32


def _const_spec(shape):
    nd = len(shape)
    return pl.BlockSpec(shape, lambda i: (0,) * nd, pipeline_mode=pl.Buffered(1))


def _params(n_grid_axes=1):
    return pltpu.CompilerParams(
        dimension_semantics=("arbitrary",) * n_grid_axes,
        vmem_limit_bytes=VMEM_LIMIT)


def _modnorm(x, g, shift, scale):
    ms = jnp.mean(x * x, axis=-1, keepdims=True)
    return (x * lax.rsqrt(ms + NORM_EPS)) * (g * (1.0 + scale)) + shift


def _silu(v):
    return v * jax.nn.sigmoid(v)


def _dot(a, b):
    return jnp.dot(a, b, preferred_element_type=_F32)


MOD_TN = 1536


def _mod_kernel(cb_ref, w_ref, b_ref, o_ref):
    s_lat = _silu(cb_ref[0])
    s_ctx = _silu(cb_ref[1])
    o_ref[...] = jnp.zeros(o_ref.shape, _F32)
    for j in range(MOD_TN // LANES):
        cols = slice(j * LANES, (j + 1) * LANES)
        w = w_ref[0, :, cols]
        bias = b_ref[0, :, cols]
        o_ref[0, 0:1, cols] = jnp.sum(w * s_lat, axis=0, keepdims=True) + bias
        o_ref[0, 1:2, cols] = jnp.sum(w * s_ctx, axis=0, keepdims=True) + bias


def _modulation(c, c_ctx, mod_w, mod_b):
    cvec = jnp.stack([c[0], c_ctx])
    cb = jnp.broadcast_to(cvec[:, :, None], (2, D_MODEL, LANES))
    n_out = 6 * D_MODEL
    return pl.pallas_call(
        _mod_kernel,
        out_shape=jax.ShapeDtypeStruct((DEPTH, 8, n_out), _F32),
        grid=(DEPTH, n_out // MOD_TN),
        in_specs=[
            pl.BlockSpec((2, D_MODEL, LANES), lambda i, j: (0, 0, 0)),
            pl.BlockSpec((1, D_MODEL, MOD_TN), lambda i, j: (i, 0, j)),
            pl.BlockSpec((1, 1, MOD_TN), lambda i, j: (i, 0, j)),
        ],
        out_specs=pl.BlockSpec((1, 8, MOD_TN), lambda i, j: (i, 0, j)),
        compiler_params=_params(2),
        name="modulation",
    )(cb, mod_w, mod_b.reshape(DEPTH, 1, n_out))


def _ffn_kernel(x_ref, m_ref, g_ref, win_ref, wout_ref, fg_ref, o_ref, *, final):
    x = x_ref[...]
    h = _modnorm(x, g_ref[...], m_ref[3:4, :], m_ref[4:5, :]).astype(_BF16)
    gt = _dot(h, win_ref[:, :D_FF])
    up = _dot(h, win_ref[:, D_FF:])
    a = (_silu(gt) * up).astype(_BF16)
    y = x + m_ref[5:6, :] * _dot(a, wout_ref[...])
    if final:
        ms = jnp.mean(y * y, axis=-1, keepdims=True)
        y = (y * lax.rsqrt(ms + NORM_EPS)) * fg_ref[...]
    o_ref[...] = y


def _ffn(x, m, g, w_in, w_out, fg, *, final, tm):
    t = x.shape[0]
    row = pl.BlockSpec((tm, D_MODEL), lambda i: (i, 0))
    return pl.pallas_call(
        functools.partial(_ffn_kernel, final=final),
        out_shape=jax.ShapeDtypeStruct((t, D_MODEL), _F32),
        grid=(t // tm,),
        in_specs=[row, _const_spec((6, D_MODEL)), _const_spec((1, D_MODEL)),
                  _const_spec((D_MODEL, 2 * D_FF)), _const_spec((D_FF, D_MODEL)),
                  _const_spec((1, D_MODEL))],
        out_specs=row,
        compiler_params=_params(),
        name="ffn",
    )(x, m, g, w_in, w_out, fg)


def _conv_kernel(x_ref, xp_ref, xn_ref, m_ref, g_ref, pw1_ref, b1_ref, dw_ref, dwb_ref,
                 lng_ref, lnb_ref, pw2_ref, b2_ref, o_ref, u_scr, c_scr, *, tm, t_total):
    i = pl.program_id(0)
    x = x_ref[...]
    xa = jnp.concatenate([xp_ref[...], x, xn_ref[...]], axis=0)
    h = _modnorm(xa, g_ref[...], m_ref[0:1, :], m_ref[1:2, :]).astype(_BF16)
    z = _dot(h, pw1_ref[...]) + b1_ref[...]
    u = z[:, :D_MODEL] * jax.nn.sigmoid(z[:, D_MODEL:])
    r = i * tm - HALO + lax.broadcasted_iota(jnp.int32, (tm + 2 * HALO, 1), 0)
    u_scr[...] = jnp.where((r >= 0) & (r < t_total), u, 0.0)

    span = CONV_RB + 2 * HALO
    for lt in range(D_MODEL // LANES):
        cols = slice(lt * LANES, (lt + 1) * LANES)
        taps = [dw_ref[k:k + 1, cols] for k in range(CONV_KERNEL)]
        bias = jnp.broadcast_to(dwb_ref[:, cols], (CONV_RB, LANES))
        for rb in range(tm // CONV_RB):
            src = u_scr[rb * CONV_RB:rb * CONV_RB + span, cols]
            acc = bias
            for b in range(8):
                ub = src if b == 0 else pltpu.roll(src, span - b, axis=0)
                for a in range(2 * HALO // 8):
                    k = 8 * a + b - (HALO - CONV_PAD)
                    if 0 <= k < CONV_KERNEL:
                        acc = acc + ub[8 * a:8 * a + CONV_RB, :] * taps[k]
            c_scr[rb * CONV_RB:(rb + 1) * CONV_RB, cols] = acc

    acc = c_scr[...]
    mu = jnp.mean(acc, axis=-1, keepdims=True)
    d = acc - mu
    var = jnp.mean(d * d, axis=-1, keepdims=True)
    v = _silu(d * lax.rsqrt(var + LN_EPS) * lng_ref[...] + lnb_ref[...]).astype(_BF16)
    y = _dot(v, pw2_ref[...]) + b2_ref[...]
    o_ref[...] = x + m_ref[2:3, :] * y


def _conv_mixer(x, m, g, pw1, b1, dw, dwb, lng, lnb, pw2, b2, *, tm):
    t = x.shape[0]
    hb = tm // HALO
    n_hb = t // HALO
    row = pl.BlockSpec((tm, D_MODEL), lambda i: (i, 0))
    prev = pl.BlockSpec((HALO, D_MODEL), lambda i: (jnp.maximum(i * hb - 1, 0), 0))
    nxt = pl.BlockSpec((HALO, D_MODEL), lambda i: (jnp.minimum((i + 1) * hb, n_hb - 1), 0))
    vec = _const_spec((1, D_MODEL))
    return pl.pallas_call(
        functools.partial(_conv_kernel, tm=tm, t_total=t),
        out_shape=jax.ShapeDtypeStruct((t, D_MODEL), _F32),
        grid=(t // tm,),
        in_specs=[row, prev, nxt, _const_spec((6, D_MODEL)), vec,
                  _const_spec((D_MODEL, 2 * D_MODEL)), _const_spec((1, 2 * D_MODEL)),
                  _const_spec((CONV_KERNEL, D_MODEL)), vec, vec, vec,
                  _const_spec((D_MODEL, D_MODEL)), vec],
        out_specs=row,
        scratch_shapes=[pltpu.VMEM((tm + 2 * HALO, D_MODEL), _F32),
                        pltpu.VMEM((tm, D_MODEL), _F32)],
        compiler_params=_params(),
        name="conv_mixer",
    )(x, x, x, m, g, pw1, b1, dw, dwb, lng, lnb, pw2, b2)


def _decay_tables(le_ref, dmat_scr, qd_scr, kd_scr, cd_scr, *, reverse):
    c = RET_CHUNK
    ri = lax.broadcasted_iota(jnp.int32, (c, c), 0)
    ci = lax.broadcasted_iota(jnp.int32, (c, c), 1)
    rel = (ci - ri) if reverse else (ri - ci)
    relf = jnp.maximum(rel, 0).astype(_F32)
    pos = lax.broadcasted_iota(jnp.int32, (c, RET_DV), 0).astype(_F32)
    posk = lax.broadcasted_iota(jnp.int32, (c, RET_DK), 0).astype(_F32)
    for hd in range(RET_HEADS):
        log_g = jnp.log(1.0 - jnp.exp2(le_ref[hd]))
        log_gk = jnp.log(1.0 - jnp.exp2(le_ref[hd, :, :RET_DK]))
        log_gc = jnp.log(1.0 - jnp.exp2(le_ref[hd, :, :c]))
        dmat_scr[hd] = jnp.where(rel >= 0, jnp.exp(relf * log_gc), 0.0)
        if reverse:
            qd_scr[hd] = jnp.exp((c - pos) * log_g)
            kd_scr[hd] = jnp.exp(posk * log_gk)
        else:
            qd_scr[hd] = jnp.exp((pos + 1.0) * log_g)
            kd_scr[hd] = jnp.exp((c - 1.0 - posk) * log_gk)
        cd_scr[hd] = jnp.exp(float(c) * log_g)


def _retention_chunk(q, k, v, hd, state_ref, dmat_scr, qd_scr, kd_scr, cd_scr):
    s = lax.dot_general(q, k, (((1,), (1,)), ((), ())), preferred_element_type=_F32)
    s = (s * dmat_scr[hd]).astype(_BF16)
    st = state_ref[hd]
    o = _dot(s, v) + _dot(q, st.astype(_BF16)) * qd_scr[hd]
    kd = (k.astype(_F32) * kd_scr[hd]).astype(_BF16)
    upd = lax.dot_general(kd, v, (((0,), (0,)), ((), ())), preferred_element_type=_F32)
    state_ref[hd] = st * cd_scr[hd] + upd
    return o


def _head_norm(o):
    return o * lax.rsqrt(jnp.mean(o * o, axis=-1, keepdims=True) + NORM_EPS)


def _ret_fwd_kernel(*refs, tm, rope):
    if rope:
        (x_ref, rcos_ref, rsin_ref, ccos_ref, csin_ref, m_ref, g_ref, w_ref, le_ref, s0_ref,
         q_ref, k_ref, v_ref, y_ref, state_ref,
         dmat_scr, qd_scr, kd_scr, cd_scr, gate_scr) = refs
    else:
        (x_ref, m_ref, g_ref, w_ref, le_ref, s0_ref,
         q_ref, k_ref, v_ref, y_ref, state_ref,
         dmat_scr, qd_scr, kd_scr, cd_scr, gate_scr) = refs

    @pl.when(pl.program_id(0) == 0)
    def _():
        _decay_tables(le_ref, dmat_scr, qd_scr, kd_scr, cd_scr, reverse=False)
        state_ref[...] = s0_ref[...]

    h = _modnorm(x_ref[...], g_ref[...], m_ref[0:1, :], m_ref[1:2, :]).astype(_BF16)
    qf = _dot(h, w_ref[:, :RET_QK_W])
    kf = _dot(h, w_ref[:, RET_QK_W:2 * RET_QK_W])
    k_scale = RET_DK ** -0.5
    for mb in range(tm // GRID_W):
        rows = slice(mb * GRID_W, (mb + 1) * GRID_W)
        if rope:
            row_cs = jnp.broadcast_to(rcos_ref[mb:mb + 1, :], (GRID_W, LANES))
            row_sn = jnp.broadcast_to(rsin_ref[mb:mb + 1, :], (GRID_W, LANES))
        for l in range(RET_QK_W // LANES):
            cols = slice(l * LANES, (l + 1) * LANES)
            qt = qf[rows, cols]
            kt = kf[rows, cols]
            if rope:
                cs = row_cs if l % 2 == 0 else ccos_ref[...]
                sn = row_sn if l % 2 == 0 else csin_ref[...]
                qt = qt * cs + pltpu.roll(qt, LANES // 2, axis=1) * sn
                kt = kt * cs + pltpu.roll(kt, LANES // 2, axis=1) * sn
            q_ref[rows, cols] = qt.astype(_BF16)
            k_ref[rows, cols] = (kt * k_scale).astype(_BF16)
    v_ref[...] = _dot(h, w_ref[:, 2 * RET_QK_W:2 * RET_QK_W + RET_V_W]).astype(_BF16)
    gate_scr[...] = _silu(_dot(h, w_ref[:, 2 * RET_QK_W + RET_V_W:]))

    for j in range(tm // RET_CHUNK):
        rows = slice(j * RET_CHUNK, (j + 1) * RET_CHUNK)
        for hd in range(RET_HEADS):
            qk_cols = slice(hd * RET_DK, (hd + 1) * RET_DK)
            v_cols = slice(hd * RET_DV, (hd + 1) * RET_DV)
            o = _retention_chunk(q_ref[rows, qk_cols], k_ref[rows, qk_cols], v_ref[rows, v_cols],
                                 hd, state_ref, dmat_scr, qd_scr, kd_scr, cd_scr)
            y_ref[rows, v_cols] = (gate_scr[rows, v_cols] * _head_norm(o)).astype(_BF16)


def _ret_bwd_kernel(x_ref, q_ref, k_ref, v_ref, yf_ref, m_ref, g_ref, wg_ref, wo_ref, le_ref,
                    s0_ref, o_ref, state_ref,
                    dmat_scr, qd_scr, kd_scr, cd_scr, gate_scr, y_scr, *, tm):
    @pl.when(pl.program_id(0) == 0)
    def _():
        _decay_tables(le_ref, dmat_scr, qd_scr, kd_scr, cd_scr, reverse=True)
        state_ref[...] = s0_ref[...]

    x = x_ref[...]
    h = _modnorm(x, g_ref[...], m_ref[0:1, :], m_ref[1:2, :]).astype(_BF16)
    gate_scr[...] = _silu(_dot(h, wg_ref[...]))

    for j in reversed(range(tm // RET_CHUNK)):
        rows = slice(j * RET_CHUNK, (j + 1) * RET_CHUNK)
        for hd in range(RET_HEADS):
            qk_cols = slice(hd * RET_DK, (hd + 1) * RET_DK)
            v_cols = slice(hd * RET_DV, (hd + 1) * RET_DV)
            o = _retention_chunk(q_ref[rows, qk_cols], k_ref[rows, qk_cols], v_ref[rows, v_cols],
                                 hd, state_ref, dmat_scr, qd_scr, kd_scr, cd_scr)
            y = yf_ref[rows, v_cols].astype(_F32) + gate_scr[rows, v_cols] * _head_norm(o)
            y_scr[rows, v_cols] = y.astype(_BF16)
    o_ref[...] = x + m_ref[2:3, :] * _dot(y_scr[...], wo_ref[...])


def _decay_scratch():
    return [pltpu.VMEM((RET_HEADS, RET_CHUNK, RET_CHUNK), _F32),
            pltpu.VMEM((RET_HEADS, RET_CHUNK, RET_DV), _F32),
            pltpu.VMEM((RET_HEADS, RET_CHUNK, RET_DK), _F32),
            pltpu.VMEM((RET_HEADS, 1, RET_DV), _F32)]


def _ret_fwd(x, rope_tabs, m, g, w_in, le, s0, *, tm):
    t = x.shape[0]
    row = lambda w: pl.BlockSpec((tm, w), lambda i: (i, 0))
    state_shape = (RET_HEADS, RET_DK, RET_DV)
    rope = rope_tabs is not None
    rope_specs = []
    if rope:
        row_tab = pl.BlockSpec((tm // GRID_W, LANES), lambda i: (i, 0))
        rope_specs = [row_tab, row_tab, _const_spec((GRID_W, LANES)), _const_spec((GRID_W, LANES))]
    w_fwd_spec = pl.BlockSpec((D_MODEL, 2 * RET_QK_W + 2 * RET_V_W), lambda i: (0, 0),
                              pipeline_mode=pl.Buffered(1))
    return pl.pallas_call(
        functools.partial(_ret_fwd_kernel, tm=tm, rope=rope),
        out_shape=(jax.ShapeDtypeStruct((t, RET_QK_W), _BF16),
                   jax.ShapeDtypeStruct((t, RET_QK_W), _BF16),
                   jax.ShapeDtypeStruct((t, RET_V_W), _BF16),
                   jax.ShapeDtypeStruct((t, RET_V_W), _BF16),
                   jax.ShapeDtypeStruct(state_shape, _F32)),
        grid=(t // tm,),
        in_specs=[row(D_MODEL)] + rope_specs + [
            _const_spec((6, D_MODEL)), _const_spec((1, D_MODEL)), w_fwd_spec,
            _const_spec((RET_HEADS, 1, RET_DV)), _const_spec(state_shape)],
        out_specs=(row(RET_QK_W), row(RET_QK_W), row(RET_V_W), row(RET_V_W),
                   pl.BlockSpec(state_shape, lambda i: (0, 0, 0))),
        scratch_shapes=_decay_scratch() + [pltpu.VMEM((tm, RET_V_W), _F32)],
        compiler_params=_params(),
        name="retention_fwd",
    )(x, *(rope_tabs or ()), m, g, w_in, le, s0)


def _ret_bwd(x, q, k, v, yf, m, g, w_in, w_out, le, s0, *, tm):
    t = x.shape[0]
    n = t // tm
    row = lambda w: pl.BlockSpec((tm, w), lambda i: (n - 1 - i, 0))
    state_shape = (RET_HEADS, RET_DK, RET_DV)
    gb_block = w_in.shape[1] // RET_V_W - 1
    w_gb_spec = pl.BlockSpec((D_MODEL, RET_V_W), lambda i: (0, gb_block),
                             pipeline_mode=pl.Buffered(1))
    return pl.pallas_call(
        functools.partial(_ret_bwd_kernel, tm=tm),
        out_shape=(jax.ShapeDtypeStruct((t, D_MODEL), _F32),
                   jax.ShapeDtypeStruct(state_shape, _F32)),
        grid=(n,),
        in_specs=[row(D_MODEL), row(RET_QK_W), row(RET_QK_W), row(RET_V_W), row(RET_V_W),
                  _const_spec((6, D_MODEL)), _const_spec((1, D_MODEL)),
                  w_gb_spec, _const_spec(w_out.shape),
                  _const_spec((RET_HEADS, 1, RET_DV)), _const_spec(state_shape)],
        out_specs=(row(D_MODEL), pl.BlockSpec(state_shape, lambda i: (0, 0, 0))),
        scratch_shapes=_decay_scratch() + [pltpu.VMEM((tm, RET_V_W), _F32),
                                           pltpu.VMEM((tm, RET_V_W), _BF16)],
        compiler_params=_params(),
        name="retention_bwd",
    )(x, q, k, v, yf, m, g, w_in, w_out, le, s0)


def _rope_tables(n_tok):
    quarter = RET_DK // 4
    inv = ROPE_BASE ** (-jnp.arange(quarter, dtype=_F32) / quarter)
    ar = jnp.arange(n_tok // GRID_W, dtype=_F32)[:, None] * inv
    ac = jnp.arange(GRID_W, dtype=_F32)[:, None] * inv
    pair = lambda a, b: jnp.concatenate([a, b], axis=-1)
    return (pair(jnp.cos(ar), jnp.cos(ar)), pair(-jnp.sin(ar), jnp.sin(ar)),
            pair(jnp.cos(ac), jnp.cos(ac)), pair(-jnp.sin(ac), jnp.sin(ac)))


def kernel(x, c, ctx, c_ctx, mod_w, mod_b, norm1_g, norm2_g, conv_pw1_w, conv_pw1_b, conv_dw_w,
           conv_dw_b, conv_ln_g, conv_ln_b, conv_pw2_w, conv_pw2_b, ret_w_in, ret_log2_eps,
           ret_w_out, ffn_w_in, ffn_w_out, final_norm_g):
    assert x.shape[0] == 1 and x.shape[2] == D_MODEL
    n_tok = x.shape[1]
    n_ctx = ctx.shape[1]
    tm = 512
    assert n_tok % tm == 0 and n_ctx % RET_CHUNK == 0
    xs = x[0]
    cs = ctx[0]
    vec = lambda a: a.reshape(1, -1)

    mods = _modulation(c, c_ctx, mod_w, mod_b)
    rope_tabs = _rope_tables(n_tok)
    zero_state = jnp.zeros((RET_HEADS, RET_DK, RET_DV), _F32)
    fg = vec(final_norm_g)

    for i in range(DEPTH):
        last = i == DEPTH - 1
        j = i // N_MIXERS
        m_lat = mods[i, 0].reshape(6, D_MODEL)
        m_ctx = mods[i, 1].reshape(6, D_MODEL)
        g1 = vec(norm1_g[i])
        g2 = vec(norm2_g[i])
        w_in = ffn_w_in[i].astype(_BF16)
        w_out = ffn_w_out[i].astype(_BF16)

        if i % N_MIXERS == 0:
            conv_args = (conv_pw1_w[j].astype(_BF16), vec(conv_pw1_b[j]), conv_dw_w[j],
                         vec(conv_dw_b[j]), vec(conv_ln_g[j]), vec(conv_ln_b[j]),
                         conv_pw2_w[j].astype(_BF16), vec(conv_pw2_b[j]))
            xs = _conv_mixer(xs, m_lat, g1, *conv_args, tm=tm)
            if not last:
                cs = _conv_mixer(cs, m_ctx, g1, *conv_args, tm=n_ctx)
        else:
            w_ri = ret_w_in[j].astype(_BF16)
            w_o = ret_w_out[j].astype(_BF16)
            le = jnp.broadcast_to(ret_log2_eps[j][:, :, None, None], (2, RET_HEADS, 1, RET_DV))
            cq, ck, cv, cy, s_f = _ret_fwd(cs, None, m_ctx, g1, w_ri, le[0], zero_state, tm=n_ctx)
            cs_new, s_b = _ret_bwd(cs, cq, ck, cv, cy, m_ctx, g1, w_ri, w_o, le[1], zero_state,
                                   tm=n_ctx)
            lq, lk, lv, ly, _ = _ret_fwd(xs, rope_tabs, m_lat, g1, w_ri, le[0], s_f, tm=tm)
            xs, _ = _ret_bwd(xs, lq, lk, lv, ly, m_lat, g1, w_ri, w_o, le[1], s_b, tm=tm)
            if not last:
                cs = cs_new

        xs = _ffn(xs, m_lat, g2, w_in, w_out, fg, final=last, tm=tm)
        if not last:
            cs = _ffn(cs, m_ctx, g2, w_in, w_out, fg, final=False, tm=n_ctx)

    return xs[None]
```

```python
import functools

import jax
import jax.numpy as jnp
from jax import lax
from jax.experimental import pallas as pl
from jax.experimental.pallas import tpu as pltpu

D_MODEL = 1024
DEPTH = 4
GRID_W = 64
N_MIXERS = 2
CONV_KERNEL = 31
CONV_PAD = CONV_KERNEL // 2
RET_HEADS = 4
RET_DK = D_MODEL // RET_HEADS
RET_DV = 2 * RET_DK
RET_QK_W = RET_HEADS * RET_DK
RET_V_W = RET_HEADS * RET_DV
RET_CHUNK = 256
ROPE_BASE = 10000.0
D_FF = 2816
NORM_EPS = 1e-6
LN_EPS = 1e-5

LANES = 128
HALO = 16
CONV_RB = 64
VMEM_LIMIT = 56 * 1024 * 1024

_BF16 = jnp.bfloat16
_F32 = jnp.float32


def _const_spec(shape):
    nd = len(shape)
    return pl.BlockSpec(shape, lambda i: (0,) * nd, pipeline_mode=pl.Buffered(1))


def _layer_spec(stacked, layer):
    return pl.BlockSpec((None,) + stacked.shape[1:], lambda i: (layer,) + (0,) * (stacked.ndim - 1),
                        pipeline_mode=pl.Buffered(1))


def _params(n_grid_axes=1):
    return pltpu.CompilerParams(
        dimension_semantics=("arbitrary",) * n_grid_axes,
        vmem_limit_bytes=VMEM_LIMIT)


def _modnorm(x, g, shift, scale):
    ms = jnp.mean(x * x, axis=-1, keepdims=True)
    return (x * lax.rsqrt(ms + NORM_EPS)) * (g * (1.0 + scale)) + shift


def _silu(v):
    return v * jax.nn.sigmoid(v)


def _dot(a, b):
    return jnp.dot(a, b, preferred_element_type=_F32)


MOD_TN = 1536


def _mod_kernel(cb_ref, w_ref, b_ref, o_ref):
    s_lat = _silu(cb_ref[0])
    s_ctx = _silu(cb_ref[1])
    o_ref[...] = jnp.zeros(o_ref.shape, _F32)
    for j in range(MOD_TN // LANES):
        cols = slice(j * LANES, (j + 1) * LANES)
        w = w_ref[0, :, cols]
        bias = b_ref[0, :, cols]
        o_ref[0, 0:1, cols] = jnp.sum(w * s_lat, axis=0, keepdims=True) + bias
        o_ref[0, 1:2, cols] = jnp.sum(w * s_ctx, axis=0, keepdims=True) + bias


def _modulation(c, c_ctx, mod_w, mod_b):
    cvec = jnp.stack([c[0], c_ctx])
    cb = jnp.broadcast_to(cvec[:, :, None], (2, D_MODEL, LANES))
    n_out = 6 * D_MODEL
    return pl.pallas_call(
        _mod_kernel,
        out_shape=jax.ShapeDtypeStruct((DEPTH, 8, n_out), _F32),
        grid=(DEPTH, n_out // MOD_TN),
        in_specs=[
            pl.BlockSpec((2, D_MODEL, LANES), lambda i, j: (0, 0, 0)),
            pl.BlockSpec((1, D_MODEL, MOD_TN), lambda i, j: (i, 0, j)),
            pl.BlockSpec((1, 1, MOD_TN), lambda i, j: (i, 0, j)),
        ],
        out_specs=pl.BlockSpec((1, 8, MOD_TN), lambda i, j: (i, 0, j)),
        compiler_params=_params(2),
        name="modulation",
    )(cb, mod_w, mod_b.reshape(DEPTH, 1, n_out))


def _ffn_body(x, m_ref, g_ref, win_ref, wout_ref, splits):
    h = _modnorm(x, g_ref[...], m_ref[3:4, :], m_ref[4:5, :]).astype(_BF16)
    acc = None
    for c0, c1 in splits:
        gt = _dot(h, win_ref[:, c0:c1])
        up = _dot(h, win_ref[:, D_FF + c0:D_FF + c1])
        a = (_silu(gt) * up).astype(_BF16)
        part = _dot(a, wout_ref[c0:c1, :])
        acc = part if acc is None else acc + part
    return x + m_ref[5:6, :] * acc


def _ffn_kernel(x_ref, m_ref, g_ref, win_ref, wout_ref, fg_ref, o_ref, *, final):
    y = _ffn_body(x_ref[...], m_ref, g_ref, win_ref, wout_ref, ((0, D_FF),))
    if final:
        ms = jnp.mean(y * y, axis=-1, keepdims=True)
        y = (y * lax.rsqrt(ms + NORM_EPS)) * fg_ref[...]
    o_ref[...] = y


def _ffn(x, m, g, w_in, w_out, layer, fg, *, final, tm):
    t = x.shape[0]
    row = pl.BlockSpec((tm, D_MODEL), lambda i: (i, 0))
    return pl.pallas_call(
        functools.partial(_ffn_kernel, final=final),
        out_shape=jax.ShapeDtypeStruct((t, D_MODEL), _F32),
        grid=(t // tm,),
        in_specs=[row, _const_spec((6, D_MODEL)), _const_spec((1, D_MODEL)),
                  _layer_spec(w_in, layer), _layer_spec(w_out, layer),
                  _const_spec((1, D_MODEL))],
        out_specs=row,
        compiler_params=_params(),
        name="ffn",
    )(x, m, g, w_in, w_out, fg)


MXU_W = 256


def _zero_after(v):
    bits = pltpu.bitcast(v, jnp.uint32)
    return pltpu.bitcast((bits >> 16) >> 16, _F32)


def _dw_block(u_scr, c_scr, dw_ref, dwb_ref, rb, lt, after=None):
    span = CONV_RB + 2 * HALO
    cols = slice(lt * LANES, (lt + 1) * LANES)
    src = u_scr[rb * CONV_RB:rb * CONV_RB + span, cols]
    bias = dwb_ref[:, cols]
    if after is not None:
        bias = bias + _zero_after(after)
    acc = jnp.broadcast_to(bias, (CONV_RB, LANES))
    for b in range(8):
        ub = src if b == 0 else pltpu.roll(src, span - b, axis=0)
        for a in range(2 * HALO // 8):
            k = 8 * a + b - (HALO - CONV_PAD)
            if 0 <= k < CONV_KERNEL:
                acc = acc + ub[8 * a:8 * a + CONV_RB, :] * dw_ref[k:k + 1, cols]
    c_scr[rb * CONV_RB:(rb + 1) * CONV_RB, cols] = acc


def _conv_ffn_kernel(x_ref, xp_ref, xn_ref, m_ref, g1_ref, pw1_ref, b1_ref, dw_ref, dwb_ref,
                     lng_ref, lnb_ref, pw2_ref, b2_ref, g2_ref, win_ref, wout_ref,
                     o_ref, u_scr, c_scr, mid_scr, ffn_in_scr, h1_scr, h2_scr, acc_scr,
                     *, tm, t_total, n_tiles):
    i = pl.program_id(0)
    tile = jnp.minimum(i, n_tiles - 1)

    @pl.when(i == 0)
    def _():
        mid_scr[...] = jnp.zeros(mid_scr.shape, _F32)

    ffn_in_scr[...] = mid_scr[...]
    h2_scr[...] = _modnorm(ffn_in_scr[...], g2_ref[...], m_ref[3:4, :], m_ref[4:5, :]).astype(_BF16)
    xa = jnp.concatenate([xp_ref[...], x_ref[...], xn_ref[...]], axis=0)
    h1_scr[...] = _modnorm(xa, g1_ref[...], m_ref[0:1, :], m_ref[1:2, :]).astype(_BF16)
    r = tile * tm - HALO + lax.broadcasted_iota(jnp.int32, (tm + 2 * HALO, 1), 0)
    valid = (r >= 0) & (r < t_total)

    def glu_group(g):
        cols = slice(g * MXU_W, (g + 1) * MXU_W)
        gcols = slice(D_MODEL + g * MXU_W, D_MODEL + (g + 1) * MXU_W)
        za = _dot(h1_scr[...], pw1_ref[:, cols]) + b1_ref[:, cols]
        zg = _dot(h1_scr[...], pw1_ref[:, gcols]) + b1_ref[:, gcols]
        u_scr[:, cols] = jnp.where(valid, za * jax.nn.sigmoid(zg), 0.0)

    def ffn_piece(p):
        cols = slice(p * MXU_W, (p + 1) * MXU_W)
        ucols = slice(D_FF + p * MXU_W, D_FF + (p + 1) * MXU_W)
        gt = _dot(h2_scr[...], win_ref[:, cols])
        up = _dot(h2_scr[...], win_ref[:, ucols])
        part = _dot((_silu(gt) * up).astype(_BF16), wout_ref[cols, :])
        acc_scr[...] = part if p == 0 else acc_scr[...] + part
        return part[0:1, 0:LANES]

    def conv_tail():
        acc = c_scr[...]
        mu = jnp.mean(acc, axis=-1, keepdims=True)
        d = acc - mu
        var = jnp.mean(d * d, axis=-1, keepdims=True)
        v = _silu(d * lax.rsqrt(var + LN_EPS) * lng_ref[...] + lnb_ref[...]).astype(_BF16)
        y = _dot(v, pw2_ref[...]) + b2_ref[...]
        mid_scr[...] = x_ref[...] + m_ref[2:3, :] * y

    n_groups = D_MODEL // MXU_W
    lt_per_group = MXU_W // LANES
    n_rb = tm // CONV_RB
    n_ffn = D_FF // MXU_W
    n_dw = n_groups * lt_per_group * n_rb
    dw_per_ffn = -(-n_dw // n_ffn)
    glu_group(0)
    done_dw = 0
    next_ffn = 0
    pace = None
    for g in range(n_groups):
        for l in range(lt_per_group):
            for rb in range(n_rb):
                _dw_block(u_scr, c_scr, dw_ref, dwb_ref, rb, g * lt_per_group + l, after=pace)
                done_dw += 1
                if l == 0 and rb == 1 and g + 1 < n_groups:
                    glu_group(g + 1)
                if done_dw % dw_per_ffn == 0 and next_ffn < n_ffn - 1:
                    pace = ffn_piece(next_ffn)
                    next_ffn += 1
    conv_tail()
    for p in range(next_ffn, n_ffn):
        ffn_piece(p)
    o_ref[...] = ffn_in_scr[...] + m_ref[5:6, :] * acc_scr[...]


def _conv_ffn(x, m, g1, g2, conv_w, layer_c, w_in, w_out, layer_f, *, tm):
    pw1, b1, dw, dwb, lng, lnb, pw2, b2 = conv_w
    t = x.shape[0]
    n = t // tm
    hb = tm // HALO
    n_hb = t // HALO
    tile = lambda i: jnp.minimum(i, n - 1)
    row = pl.BlockSpec((tm, D_MODEL), lambda i: (tile(i), 0))
    prev = pl.BlockSpec((HALO, D_MODEL), lambda i: (jnp.maximum(tile(i) * hb - 1, 0), 0))
    nxt = pl.BlockSpec((HALO, D_MODEL), lambda i: (jnp.minimum((tile(i) + 1) * hb, n_hb - 1), 0))
    out = pl.BlockSpec((tm, D_MODEL), lambda i: (jnp.maximum(i - 1, 0), 0))
    vec = _const_spec((1, D_MODEL))
    lvec = lambda a: _layer_spec(a, layer_c)
    return pl.pallas_call(
        functools.partial(_conv_ffn_kernel, tm=tm, t_total=t, n_tiles=n),
        out_shape=jax.ShapeDtypeStruct((t, D_MODEL), _F32),
        grid=(n + 1,),
        in_specs=[row, prev, nxt, _const_spec((6, D_MODEL)), vec,
                  lvec(pw1), lvec(b1), lvec(dw), lvec(dwb), lvec(lng), lvec(lnb), lvec(pw2),
                  lvec(b2), vec, _layer_spec(w_in, layer_f), _layer_spec(w_out, layer_f)],
        out_specs=out,
        scratch_shapes=[pltpu.VMEM((tm + 2 * HALO, D_MODEL), _F32),
                        pltpu.VMEM((tm, D_MODEL), _F32),
                        pltpu.VMEM((tm, D_MODEL), _F32),
                        pltpu.VMEM((tm, D_MODEL), _F32),
                        pltpu.VMEM((tm + 2 * HALO, D_MODEL), _BF16),
                        pltpu.VMEM((tm, D_MODEL), _BF16),
                        pltpu.VMEM((tm, D_MODEL), _F32)],
        compiler_params=_params(),
        name="conv_ffn",
    )(x, x, x, m, g1, pw1, b1, dw, dwb, lng, lnb, pw2, b2, g2, w_in, w_out)


def _decay_tables(le_ref, dmat_scr, qd_scr, kd_scr, cd_scr, *, reverse):
    c = RET_CHUNK
    ri = lax.broadcasted_iota(jnp.int32, (c, c), 0)
    ci = lax.broadcasted_iota(jnp.int32, (c, c), 1)
    rel = (ci - ri) if reverse else (ri - ci)
    relf = jnp.maximum(rel, 0).astype(_F32)
    pos = lax.broadcasted_iota(jnp.int32, (c, RET_DV), 0).astype(_F32)
    posk = lax.broadcasted_iota(jnp.int32, (c, RET_DK), 0).astype(_F32)
    for hd in range(RET_HEADS):
        log_g = jnp.log(1.0 - jnp.exp2(le_ref[hd]))
        log_gk = jnp.log(1.0 - jnp.exp2(le_ref[hd, :, :RET_DK]))
        log_gc = jnp.log(1.0 - jnp.exp2(le_ref[hd, :, :c]))
        dmat_scr[hd] = jnp.where(rel >= 0, jnp.exp(relf * log_gc), 0.0)
        if reverse:
            qd_scr[hd] = jnp.exp((c - pos) * log_g)
            kd_scr[hd] = jnp.exp(posk * log_gk)
        else:
            qd_scr[hd] = jnp.exp((pos + 1.0) * log_g)
            kd_scr[hd] = jnp.exp((c - 1.0 - posk) * log_gk)
        cd_scr[hd] = jnp.exp(float(c) * log_g)


def _retention_chunk(q, k, v, hd, state_ref, dmat_scr, qd_scr, kd_scr, cd_scr):
    s = lax.dot_general(q, k, (((1,), (1,)), ((), ())), preferred_element_type=_F32)
    s = (s * dmat_scr[hd]).astype(_BF16)
    st = state_ref[hd]
    o = _dot(s, v) + _dot(q, st.astype(_BF16)) * qd_scr[hd]
    kd = (k.astype(_F32) * kd_scr[hd]).astype(_BF16)
    upd = lax.dot_general(kd, v, (((0,), (0,)), ((), ())), preferred_element_type=_F32)
    state_ref[hd] = st * cd_scr[hd] + upd
    return o


def _head_norm(o):
    return o * lax.rsqrt(jnp.mean(o * o, axis=-1, keepdims=True) + NORM_EPS)


def _ret_fwd_kernel(*refs, tm, rope):
    if rope:
        (x_ref, rcos_ref, rsin_ref, ccos_ref, csin_ref, m_ref, g_ref, w_ref, le_ref, s0_ref,
         q_ref, k_ref, v_ref, y_ref, state_ref,
         dmat_scr, qd_scr, kd_scr, cd_scr, gate_scr) = refs
    else:
        (x_ref, m_ref, g_ref, w_ref, le_ref, s0_ref,
         q_ref, k_ref, v_ref, y_ref, state_ref,
         dmat_scr, qd_scr, kd_scr, cd_scr, gate_scr) = refs

    @pl.when(pl.program_id(0) == 0)
    def _():
        _decay_tables(le_ref, dmat_scr, qd_scr, kd_scr, cd_scr, reverse=False)
        state_ref[...] = s0_ref[...]

    h = _modnorm(x_ref[...], g_ref[...], m_ref[0:1, :], m_ref[1:2, :]).astype(_BF16)
    qf = _dot(h, w_ref[:, :RET_QK_W])
    kf = _dot(h, w_ref[:, RET_QK_W:2 * RET_QK_W])
    k_scale = RET_DK ** -0.5
    for mb in range(tm // GRID_W):
        rows = slice(mb * GRID_W, (mb + 1) * GRID_W)
        if rope:
            row_cs = jnp.broadcast_to(rcos_ref[mb:mb + 1, :], (GRID_W, LANES))
            row_sn = jnp.broadcast_to(rsin_ref[mb:mb + 1, :], (GRID_W, LANES))
        for l in range(RET_QK_W // LANES):
            cols = slice(l * LANES, (l + 1) * LANES)
            qt = qf[rows, cols]
            kt = kf[rows, cols]
            if rope:
                cs = row_cs if l % 2 == 0 else ccos_ref[...]
                sn = row_sn if l % 2 == 0 else csin_ref[...]
                qt = qt * cs + pltpu.roll(qt, LANES // 2, axis=1) * sn
                kt = kt * cs + pltpu.roll(kt, LANES // 2, axis=1) * sn
            q_ref[rows, cols] = qt.astype(_BF16)
            k_ref[rows, cols] = (kt * k_scale).astype(_BF16)
    v_ref[...] = _dot(h, w_ref[:, 2 * RET_QK_W:2 * RET_QK_W + RET_V_W]).astype(_BF16)
    gate_scr[...] = _silu(_dot(h, w_ref[:, 2 * RET_QK_W + RET_V_W:]))

    for j in range(tm // RET_CHUNK):
        rows = slice(j * RET_CHUNK, (j + 1) * RET_CHUNK)
        for hd in range(RET_HEADS):
            qk_cols = slice(hd * RET_DK, (hd + 1) * RET_DK)
            v_cols = slice(hd * RET_DV, (hd + 1) * RET_DV)
            o = _retention_chunk(q_ref[rows, qk_cols], k_ref[rows, qk_cols], v_ref[rows, v_cols],
                                 hd, state_ref, dmat_scr, qd_scr, kd_scr, cd_scr)
            y_ref[rows, v_cols] = (gate_scr[rows, v_cols] * _head_norm(o)).astype(_BF16)


def _ret_bwd_kernel(x_ref, q_ref, k_ref, v_ref, yf_ref, m_ref, g_ref, wg_ref, wo_ref, le_ref,
                    s0_ref, o_ref, state_ref,
                    dmat_scr, qd_scr, kd_scr, cd_scr, gate_scr, y_scr, *, tm):
    @pl.when(pl.program_id(0) == 0)
    def _():
        _decay_tables(le_ref, dmat_scr, qd_scr, kd_scr, cd_scr, reverse=True)
        state_ref[...] = s0_ref[...]

    x = x_ref[...]
    h = _modnorm(x, g_ref[...], m_ref[0:1, :], m_ref[1:2, :]).astype(_BF16)
    gate_scr[...] = _silu(_dot(h, wg_ref[...]))

    for j in reversed(range(tm // RET_CHUNK)):
        rows = slice(j * RET_CHUNK, (j + 1) * RET_CHUNK)
        for hd in range(RET_HEADS):
            qk_cols = slice(hd * RET_DK, (hd + 1) * RET_DK)
            v_cols = slice(hd * RET_DV, (hd + 1) * RET_DV)
            o = _retention_chunk(q_ref[rows, qk_cols], k_ref[rows, qk_cols], v_ref[rows, v_cols],
                                 hd, state_ref, dmat_scr, qd_scr, kd_scr, cd_scr)
            y = yf_ref[rows, v_cols].astype(_F32) + gate_scr[rows, v_cols] * _head_norm(o)
            y_scr[rows, v_cols] = y.astype(_BF16)
    o_ref[...] = x + m_ref[2:3, :] * _dot(y_scr[...], wo_ref[...])


def _decay_scratch():
    return [pltpu.VMEM((RET_HEADS, RET_CHUNK, RET_CHUNK), _F32),
            pltpu.VMEM((RET_HEADS, RET_CHUNK, RET_DV), _F32),
            pltpu.VMEM((RET_HEADS, RET_CHUNK, RET_DK), _F32),
            pltpu.VMEM((RET_HEADS, 1, RET_DV), _F32)]


def _ret_fwd(x, rope_tabs, m, g, w_in, layer, le, s0, *, tm):
    t = x.shape[0]
    row = lambda w: pl.BlockSpec((tm, w), lambda i: (i, 0))
    state_shape = (RET_HEADS, RET_DK, RET_DV)
    rope = rope_tabs is not None
    rope_specs = []
    if rope:
        row_tab = pl.BlockSpec((tm // GRID_W, LANES), lambda i: (i, 0))
        rope_specs = [row_tab, row_tab, _const_spec((GRID_W, LANES)), _const_spec((GRID_W, LANES))]
    w_fwd_spec = pl.BlockSpec((None, D_MODEL, 2 * RET_QK_W + 2 * RET_V_W),
                              lambda i: (layer, 0, 0), pipeline_mode=pl.Buffered(1))
    return pl.pallas_call(
        functools.partial(_ret_fwd_kernel, tm=tm, rope=rope),
        out_shape=(jax.ShapeDtypeStruct((t, RET_QK_W), _BF16),
                   jax.ShapeDtypeStruct((t, RET_QK_W), _BF16),
                   jax.ShapeDtypeStruct((t, RET_V_W), _BF16),
                   jax.ShapeDtypeStruct((t, RET_V_W), _BF16),
                   jax.ShapeDtypeStruct(state_shape, _F32)),
        grid=(t // tm,),
        in_specs=[row(D_MODEL)] + rope_specs + [
            _const_spec((6, D_MODEL)), _const_spec((1, D_MODEL)), w_fwd_spec,
            _const_spec((RET_HEADS, 1, RET_DV)), _const_spec(state_shape)],
        out_specs=(row(RET_QK_W), row(RET_QK_W), row(RET_V_W), row(RET_V_W),
                   pl.BlockSpec(state_shape, lambda i: (0, 0, 0))),
        scratch_shapes=_decay_scratch() + [pltpu.VMEM((tm, RET_V_W), _F32)],
        compiler_params=_params(),
        name="retention_fwd",
    )(x, *(rope_tabs or ()), m, g, w_in, le, s0)


def _ret_bwd(x, q, k, v, yf, m, g, w_in, w_out, layer, le, s0, *, tm):
    t = x.shape[0]
    n = t // tm
    row = lambda w: pl.BlockSpec((tm, w), lambda i: (n - 1 - i, 0))
    state_shape = (RET_HEADS, RET_DK, RET_DV)
    gb_block = w_in.shape[2] // RET_V_W - 1
    w_gb_spec = pl.BlockSpec((None, D_MODEL, RET_V_W), lambda i: (layer, 0, gb_block),
                             pipeline_mode=pl.Buffered(1))
    return pl.pallas_call(
        functools.partial(_ret_bwd_kernel, tm=tm),
        out_shape=(jax.ShapeDtypeStruct((t, D_MODEL), _F32),
                   jax.ShapeDtypeStruct(state_shape, _F32)),
        grid=(n,),
        in_specs=[row(D_MODEL), row(RET_QK_W), row(RET_QK_W), row(RET_V_W), row(RET_V_W),
                  _const_spec((6, D_MODEL)), _const_spec((1, D_MODEL)),
                  w_gb_spec, _layer_spec(w_out, layer),
                  _const_spec((RET_HEADS, 1, RET_DV)), _const_spec(state_shape)],
        out_specs=(row(D_MODEL), pl.BlockSpec(state_shape, lambda i: (0, 0, 0))),
        scratch_shapes=_decay_scratch() + [pltpu.VMEM((tm, RET_V_W), _F32),
                                           pltpu.VMEM((tm, RET_V_W), _BF16)],
        compiler_params=_params(),
        name="retention_bwd",
    )(x, q, k, v, yf, m, g, w_in, w_out, le, s0)


def _rope_tables(n_tok):
    quarter = RET_DK // 4
    inv = ROPE_BASE ** (-jnp.arange(quarter, dtype=_F32) / quarter)
    ar = jnp.arange(n_tok // GRID_W, dtype=_F32)[:, None] * inv
    ac = jnp.arange(GRID_W, dtype=_F32)[:, None] * inv
    pair = lambda a, b: jnp.concatenate([a, b], axis=-1)
    return (pair(jnp.cos(ar), jnp.cos(ar)), pair(-jnp.sin(ar), jnp.sin(ar)),
            pair(jnp.cos(ac), jnp.cos(ac)), pair(-jnp.sin(ac), jnp.sin(ac)))


def kernel(x, c, ctx, c_ctx, mod_w, mod_b, norm1_g, norm2_g, conv_pw1_w, conv_pw1_b, conv_dw_w,
           conv_dw_b, conv_ln_g, conv_ln_b, conv_pw2_w, conv_pw2_b, ret_w_in, ret_log2_eps,
           ret_w_out, ffn_w_in, ffn_w_out, final_norm_g):
    assert x.shape[0] == 1 and x.shape[2] == D_MODEL
    n_tok = x.shape[1]
    n_ctx = ctx.shape[1]
    tm = 512
    tm_conv = 256
    assert n_tok % tm == 0 and n_ctx % RET_CHUNK == 0 and n_tok % tm_conv == 0
    xs = x[0]
    cs = ctx[0]
    vec = lambda a: a.reshape(1, -1)
    lvec = lambda a: a.reshape(a.shape[0], 1, -1)

    mods = _modulation(c, c_ctx, mod_w, mod_b)
    rope_tabs = _rope_tables(n_tok)
    zero_state = jnp.zeros((RET_HEADS, RET_DK, RET_DV), _F32)
    fg = vec(final_norm_g)
    conv_w = (conv_pw1_w.astype(_BF16), lvec(conv_pw1_b), conv_dw_w, lvec(conv_dw_b),
              lvec(conv_ln_g), lvec(conv_ln_b), conv_pw2_w.astype(_BF16), lvec(conv_pw2_b))
    w_ri = ret_w_in.astype(_BF16)
    w_ro = ret_w_out.astype(_BF16)
    f_in = ffn_w_in.astype(_BF16)
    f_out = ffn_w_out.astype(_BF16)

    for i in range(DEPTH):
        last = i == DEPTH - 1
        j = i // N_MIXERS
        m_lat = mods[i, 0].reshape(6, D_MODEL)
        m_ctx = mods[i, 1].reshape(6, D_MODEL)
        g1 = vec(norm1_g[i])
        g2 = vec(norm2_g[i])

        if i % N_MIXERS == 0 and not last:
            xs = _conv_ffn(xs, m_lat, g1, g2, conv_w, j, f_in, f_out, i, tm=tm_conv)
            cs = _conv_ffn(cs, m_ctx, g1, g2, conv_w, j, f_in, f_out, i, tm=n_ctx)
            continue

        assert i % N_MIXERS == 1, "a final conv layer would need the final norm in _conv_ffn"
        le = jnp.broadcast_to(ret_log2_eps[j][:, :, None, None], (2, RET_HEADS, 1, RET_DV))
        cq, ck, cv, cy, s_f = _ret_fwd(cs, None, m_ctx, g1, w_ri, j, le[0], zero_state, tm=n_ctx)
        cs_new, s_b = _ret_bwd(cs, cq, ck, cv, cy, m_ctx, g1, w_ri, w_ro, j, le[1], zero_state,
                               tm=n_ctx)
        lq, lk, lv, ly, _ = _ret_fwd(xs, rope_tabs, m_lat, g1, w_ri, j, le[0], s_f, tm=tm)
        xs, _ = _ret_bwd(xs, lq, lk, lv, ly, m_lat, g1, w_ri, w_ro, j, le[1], s_b, tm=tm)
        xs = _ffn(xs, m_lat, g2, f_in, f_out, i, fg, final=last, tm=tm)
        if not last:
            cs = _ffn(cs_new, m_ctx, g2, f_in, f_out, i, fg, final=False, tm=n_ctx)

    return xs[None]
```

```python
import functools

import jax
import jax.numpy as jnp
import numpy as np
from jax import lax
from jax.experimental import pallas as pl
from jax.experimental.pallas import tpu as pltpu

D_MODEL = 1024
DEPTH = 4
GRID_W = 64
N_MIXERS = 2
CONV_KERNEL = 31
CONV_PAD = CONV_KERNEL // 2
RET_HEADS = 4
RET_DK = D_MODEL // RET_HEADS
RET_DV = 2 * RET_DK
RET_QK_W = RET_HEADS * RET_DK
RET_V_W = RET_HEADS * RET_DV
RET_CHUNK = 256
ROPE_BASE = 10000.0
D_FF = 2816
NORM_EPS = 1e-6
LN_EPS = 1e-5

LANES = 128
HALO = 16
CONV_RB = 64
VMEM_LIMIT = 56 * 1024 * 1024

_BF16 = jnp.bfloat16
_F32 = jnp.float32


def _const_spec(shape):
    nd = len(shape)
    return pl.BlockSpec(shape, lambda i: (0,) * nd, pipeline_mode=pl.Buffered(1))


def _layer_spec(stacked, layer):
    return pl.BlockSpec((None,) + stacked.shape[1:], lambda i: (layer,) + (0,) * (stacked.ndim - 1),
                        pipeline_mode=pl.Buffered(1))


def _params(n_grid_axes=1):
    return pltpu.CompilerParams(
        dimension_semantics=("arbitrary",) * n_grid_axes,
        vmem_limit_bytes=VMEM_LIMIT)


def _modnorm(x, g, shift, scale):
    ms = jnp.mean(x * x, axis=-1, keepdims=True)
    return (x * lax.rsqrt(ms + NORM_EPS)) * (g * (1.0 + scale)) + shift


def _silu(v):
    return v * jax.nn.sigmoid(v)


def _dot(a, b):
    return jnp.dot(a, b, preferred_element_type=_F32)


MOD_TN = 3072


def _mod_kernel(cb_ref, w_ref, b_ref, o_ref):
    s_lat = _silu(cb_ref[0])
    s_ctx = _silu(cb_ref[1])
    o_ref[...] = jnp.zeros(o_ref.shape, _F32)
    for j in range(MOD_TN // LANES):
        cols = slice(j * LANES, (j + 1) * LANES)
        w = w_ref[0, :, cols]
        bias = b_ref[0, :, cols]
        o_ref[0, 0:1, cols] = jnp.sum(w * s_lat, axis=0, keepdims=True) + bias
        o_ref[0, 1:2, cols] = jnp.sum(w * s_ctx, axis=0, keepdims=True) + bias


def _modulation(c, c_ctx, mod_w, mod_b):
    cvec = jnp.stack([c[0], c_ctx])
    cb = jnp.broadcast_to(cvec[:, :, None], (2, D_MODEL, LANES))
    n_out = 6 * D_MODEL
    return pl.pallas_call(
        _mod_kernel,
        out_shape=jax.ShapeDtypeStruct((DEPTH, 8, n_out), _F32),
        grid=(DEPTH, n_out // MOD_TN),
        in_specs=[
            pl.BlockSpec((2, D_MODEL, LANES), lambda i, j: (0, 0, 0)),
            pl.BlockSpec((1, D_MODEL, MOD_TN), lambda i, j: (i, 0, j)),
            pl.BlockSpec((1, 1, MOD_TN), lambda i, j: (i, 0, j)),
        ],
        out_specs=pl.BlockSpec((1, 8, MOD_TN), lambda i, j: (i, 0, j)),
        compiler_params=_params(2),
        name="modulation",
    )(cb, mod_w, mod_b.reshape(DEPTH, 1, n_out))


def _ffn_body(x, m_ref, g_ref, win_ref, wout_ref, splits):
    h = _modnorm(x, g_ref[...], m_ref[3:4, :], m_ref[4:5, :]).astype(_BF16)
    acc = None
    for c0, c1 in splits:
        gt = _dot(h, win_ref[:, c0:c1])
        up = _dot(h, win_ref[:, D_FF + c0:D_FF + c1])
        a = (_silu(gt) * up).astype(_BF16)
        part = _dot(a, wout_ref[c0:c1, :])
        acc = part if acc is None else acc + part
    return x + m_ref[5:6, :] * acc


def _ffn_kernel(x_ref, m_ref, g_ref, win_ref, wout_ref, fg_ref, o_ref, *, final):
    y = _ffn_body(x_ref[...], m_ref, g_ref, win_ref, wout_ref, ((0, D_FF),))
    if final:
        ms = jnp.mean(y * y, axis=-1, keepdims=True)
        y = (y * lax.rsqrt(ms + NORM_EPS)) * fg_ref[...]
    o_ref[...] = y


def _ffn(x, m, g, w_in, w_out, layer, fg, *, final, tm):
    t = x.shape[0]
    row = pl.BlockSpec((tm, D_MODEL), lambda i: (i, 0))
    return pl.pallas_call(
        functools.partial(_ffn_kernel, final=final),
        out_shape=jax.ShapeDtypeStruct((t, D_MODEL), _F32),
        grid=(t // tm,),
        in_specs=[row, _const_spec((6, D_MODEL)), _const_spec((1, D_MODEL)),
                  _layer_spec(w_in, layer), _layer_spec(w_out, layer),
                  _const_spec((1, D_MODEL))],
        out_specs=row,
        compiler_params=_params(),
        name="ffn",
    )(x, m, g, w_in, w_out, fg)


CONV_SPAN = CONV_RB + 2 * HALO
CONV_SH = CONV_SPAN - 8


def _shift_matrix():
    s = np.zeros((7 * CONV_SH, 2 * CONV_SPAN), np.float32)
    for b in range(1, 8):
        for j in range(CONV_SH):
            s[(b - 1) * CONV_SH + j, j + b] = 1.0
            s[(b - 1) * CONV_SH + j, CONV_SPAN + j + b] = 1.0
    return jnp.asarray(s, _BF16)


def _conv_kernel(x_ref, xp_ref, xn_ref, m_ref, g_ref, pw1_ref, b1_ref, dw_ref, dwb_ref,
                 lng_ref, lnb_ref, pw2_ref, b2_ref, sh_ref, o_ref, u_scr, s_scr, c_scr,
                 *, tm, t_total):
    i = pl.program_id(0)
    x = x_ref[...]
    xa = jnp.concatenate([xp_ref[...], x, xn_ref[...]], axis=0)
    h = _modnorm(xa, g_ref[...], m_ref[0:1, :], m_ref[1:2, :]).astype(_BF16)
    z = _dot(h, pw1_ref[...]) + b1_ref[...]
    u = z[:, :D_MODEL] * jax.nn.sigmoid(z[:, D_MODEL:])
    r = i * tm - HALO + lax.broadcasted_iota(jnp.int32, (tm + 2 * HALO, 1), 0)
    u_scr[...] = jnp.where((r >= 0) & (r < t_total), u, 0.0)

    for rb in range(tm // CONV_RB):
        base = rb * CONV_RB
        src = u_scr[base:base + CONV_SPAN, :]
        hi = src.astype(_BF16)
        lo = (src - hi.astype(_F32)).astype(_BF16)
        s_scr[...] = _dot(sh_ref[...], jnp.concatenate([hi, lo], axis=0))
        for lt in range(D_MODEL // LANES):
            cols = slice(lt * LANES, (lt + 1) * LANES)
            acc = jnp.broadcast_to(dwb_ref[:, cols], (CONV_RB, LANES))
            for b in range(8):
                for a in range(2 * HALO // 8):
                    k = 8 * a + b - (HALO - CONV_PAD)
                    if not 0 <= k < CONV_KERNEL:
                        continue
                    if b == 0:
                        rows = u_scr[base + 8 * a:base + 8 * a + CONV_RB, cols]
                    else:
                        off = (b - 1) * CONV_SH + 8 * a
                        rows = s_scr[off:off + CONV_RB, cols]
                    acc = acc + rows * dw_ref[k:k + 1, cols]
            c_scr[base:base + CONV_RB, cols] = acc

    acc = c_scr[...]
    mu = jnp.mean(acc, axis=-1, keepdims=True)
    d = acc - mu
    var = jnp.mean(d * d, axis=-1, keepdims=True)
    v = _silu(d * lax.rsqrt(var + LN_EPS) * lng_ref[...] + lnb_ref[...]).astype(_BF16)
    y = _dot(v, pw2_ref[...]) + b2_ref[...]
    o_ref[...] = x + m_ref[2:3, :] * y


def _conv_mixer(x, m, g, conv_w, layer, *, tm):
    pw1, b1, dw, dwb, lng, lnb, pw2, b2 = conv_w
    t = x.shape[0]
    hb = tm // HALO
    n_hb = t // HALO
    row = pl.BlockSpec((tm, D_MODEL), lambda i: (i, 0))
    prev = pl.BlockSpec((HALO, D_MODEL), lambda i: (jnp.maximum(i * hb - 1, 0), 0))
    nxt = pl.BlockSpec((HALO, D_MODEL), lambda i: (jnp.minimum((i + 1) * hb, n_hb - 1), 0))
    lspec = lambda a: _layer_spec(a, layer)
    shift = _shift_matrix()
    return pl.pallas_call(
        functools.partial(_conv_kernel, tm=tm, t_total=t),
        out_shape=jax.ShapeDtypeStruct((t, D_MODEL), _F32),
        grid=(t // tm,),
        in_specs=[row, prev, nxt, _const_spec((6, D_MODEL)), _const_spec((1, D_MODEL)),
                  lspec(pw1), lspec(b1), lspec(dw), lspec(dwb), lspec(lng), lspec(lnb),
                  lspec(pw2), lspec(b2), _const_spec(shift.shape)],
        out_specs=row,
        scratch_shapes=[pltpu.VMEM((tm + 2 * HALO, D_MODEL), _F32),
                        pltpu.VMEM((7 * CONV_SH, D_MODEL), _F32),
                        pltpu.VMEM((tm, D_MODEL), _F32)],
        compiler_params=_params(),
        name="conv_mixer",
    )(x, x, x, m, g, pw1, b1, dw, dwb, lng, lnb, pw2, b2, shift)


def _decay_tables(le_ref, dmat_scr, qd_scr, kd_scr, cd_scr, *, reverse):
    c = RET_CHUNK
    ri = lax.broadcasted_iota(jnp.int32, (c, c), 0)
    ci = lax.broadcasted_iota(jnp.int32, (c, c), 1)
    rel = (ci - ri) if reverse else (ri - ci)
    relf = jnp.maximum(rel, 0).astype(_F32)
    pos = lax.broadcasted_iota(jnp.int32, (c, RET_DV), 0).astype(_F32)
    posk = lax.broadcasted_iota(jnp.int32, (c, RET_DK), 0).astype(_F32)
    for hd in range(RET_HEADS):
        log_g = jnp.log(1.0 - jnp.exp2(le_ref[hd]))
        log_gk = jnp.log(1.0 - jnp.exp2(le_ref[hd, :, :RET_DK]))
        log_gc = jnp.log(1.0 - jnp.exp2(le_ref[hd, :, :c]))
        dmat_scr[hd] = jnp.where(rel >= 0, jnp.exp(relf * log_gc), 0.0)
        if reverse:
            qd_scr[hd] = jnp.exp((c - pos) * log_g)
            kd_scr[hd] = jnp.exp(posk * log_gk)
        else:
            qd_scr[hd] = jnp.exp((pos + 1.0) * log_g)
            kd_scr[hd] = jnp.exp((c - 1.0 - posk) * log_gk)
        cd_scr[hd] = jnp.exp(float(c) * log_g)


def _retention_chunk(q, k, v, hd, state_ref, dmat_scr, qd_scr, kd_scr, cd_scr):
    s = lax.dot_general(q, k, (((1,), (1,)), ((), ())), preferred_element_type=_F32)
    s = (s * dmat_scr[hd]).astype(_BF16)
    st = state_ref[hd]
    o = _dot(s, v) + _dot(q, st.astype(_BF16)) * qd_scr[hd]
    kd = (k.astype(_F32) * kd_scr[hd]).astype(_BF16)
    upd = lax.dot_general(kd, v, (((0,), (0,)), ((), ())), preferred_element_type=_F32)
    state_ref[hd] = st * cd_scr[hd] + upd
    return o


def _head_norm(o):
    return o * lax.rsqrt(jnp.mean(o * o, axis=-1, keepdims=True) + NORM_EPS)


def _ret_fwd_kernel(*refs, tm, rope):
    if rope:
        (x_ref, rcos_ref, rsin_ref, ccos_ref, csin_ref, m_ref, g_ref, w_ref, le_ref, s0_ref,
         q_ref, k_ref, v_ref, y_ref, state_ref,
         dmat_scr, qd_scr, kd_scr, cd_scr, gate_scr) = refs
    else:
        (x_ref, m_ref, g_ref, w_ref, le_ref, s0_ref,
         q_ref, k_ref, v_ref, y_ref, state_ref,
         dmat_scr, qd_scr, kd_scr, cd_scr, gate_scr) = refs

    @pl.when(pl.program_id(0) == 0)
    def _():
        _decay_tables(le_ref, dmat_scr, qd_scr, kd_scr, cd_scr, reverse=False)
        state_ref[...] = s0_ref[...]

    h = _modnorm(x_ref[...], g_ref[...], m_ref[0:1, :], m_ref[1:2, :]).astype(_BF16)
    qf = _dot(h, w_ref[:, :RET_QK_W])
    kf = _dot(h, w_ref[:, RET_QK_W:2 * RET_QK_W])
    k_scale = RET_DK ** -0.5
    for mb in range(tm // GRID_W):
        rows = slice(mb * GRID_W, (mb + 1) * GRID_W)
        if rope:
            row_cs = jnp.broadcast_to(rcos_ref[mb:mb + 1, :], (GRID_W, LANES))
            row_sn = jnp.broadcast_to(rsin_ref[mb:mb + 1, :], (GRID_W, LANES))
        for l in range(RET_QK_W // LANES):
            cols = slice(l * LANES, (l + 1) * LANES)
            qt = qf[rows, cols]
            kt = kf[rows, cols]
            if rope:
                cs = row_cs if l % 2 == 0 else ccos_ref[...]
                sn = row_sn if l % 2 == 0 else csin_ref[...]
                qt = qt * cs + pltpu.roll(qt, LANES // 2, axis=1) * sn
                kt = kt * cs + pltpu.roll(kt, LANES // 2, axis=1) * sn
            q_ref[rows, cols] = qt.astype(_BF16)
            k_ref[rows, cols] = (kt * k_scale).astype(_BF16)
    v_ref[...] = _dot(h, w_ref[:, 2 * RET_QK_W:2 * RET_QK_W + RET_V_W]).astype(_BF16)
    gate_scr[...] = _silu(_dot(h, w_ref[:, 2 * RET_QK_W + RET_V_W:]))

    for j in range(tm // RET_CHUNK):
        rows = slice(j * RET_CHUNK, (j + 1) * RET_CHUNK)
        for hd in range(RET_HEADS):
            qk_cols = slice(hd * RET_DK, (hd + 1) * RET_DK)
            v_cols = slice(hd * RET_DV, (hd + 1) * RET_DV)
            o = _retention_chunk(q_ref[rows, qk_cols], k_ref[rows, qk_cols], v_ref[rows, v_cols],
                                 hd, state_ref, dmat_scr, qd_scr, kd_scr, cd_scr)
            y_ref[rows, v_cols] = (gate_scr[rows, v_cols] * _head_norm(o)).astype(_BF16)


def _ret_bwd_kernel(x_ref, q_ref, k_ref, v_ref, yf_ref, m_ref, g_ref, wg_ref, wo_ref, le_ref,
                    s0_ref, o_ref, state_ref,
                    dmat_scr, qd_scr, kd_scr, cd_scr, gate_scr, y_scr, *, tm):
    @pl.when(pl.program_id(0) == 0)
    def _():
        _decay_tables(le_ref, dmat_scr, qd_scr, kd_scr, cd_scr, reverse=True)
        state_ref[...] = s0_ref[...]

    x = x_ref[...]
    h = _modnorm(x, g_ref[...], m_ref[0:1, :], m_ref[1:2, :]).astype(_BF16)
    gate_scr[...] = _silu(_dot(h, wg_ref[...]))

    for j in reversed(range(tm // RET_CHUNK)):
        rows = slice(j * RET_CHUNK, (j + 1) * RET_CHUNK)
        for hd in range(RET_HEADS):
            qk_cols = slice(hd * RET_DK, (hd + 1) * RET_DK)
            v_cols = slice(hd * RET_DV, (hd + 1) * RET_DV)
            o = _retention_chunk(q_ref[rows, qk_cols], k_ref[rows, qk_cols], v_ref[rows, v_cols],
                                 hd, state_ref, dmat_scr, qd_scr, kd_scr, cd_scr)
            y = yf_ref[rows, v_cols].astype(_F32) + gate_scr[rows, v_cols] * _head_norm(o)
            y_scr[rows, v_cols] = y.astype(_BF16)
    o_ref[...] = x + m_ref[2:3, :] * _dot(y_scr[...], wo_ref[...])


def _decay_scratch():
    return [pltpu.VMEM((RET_HEADS, RET_CHUNK, RET_CHUNK), _F32),
            pltpu.VMEM((RET_HEADS, RET_CHUNK, RET_DV), _F32),
            pltpu.VMEM((RET_HEADS, RET_CHUNK, RET_DK), _F32),
            pltpu.VMEM((RET_HEADS, 1, RET_DV), _F32)]


def _ret_fwd(x, rope_tabs, m, g, w_in, layer, le, s0, *, tm):
    t = x.shape[0]
    row = lambda w: pl.BlockSpec((tm, w), lambda i: (i, 0))
    state_shape = (RET_HEADS, RET_DK, RET_DV)
    rope = rope_tabs is not None
    rope_specs = []
    if rope:
        row_tab = pl.BlockSpec((tm // GRID_W, LANES), lambda i: (i, 0))
        rope_specs = [row_tab, row_tab, _const_spec((GRID_W, LANES)), _const_spec((GRID_W, LANES))]
    w_fwd_spec = pl.BlockSpec((None, D_MODEL, 2 * RET_QK_W + 2 * RET_V_W),
                              lambda i: (layer, 0, 0), pipeline_mode=pl.Buffered(1))
    return pl.pallas_call(
        functools.partial(_ret_fwd_kernel, tm=tm, rope=rope),
        out_shape=(jax.ShapeDtypeStruct((t, RET_QK_W), _BF16),
                   jax.ShapeDtypeStruct((t, RET_QK_W), _BF16),
                   jax.ShapeDtypeStruct((t, RET_V_W), _BF16),
                   jax.ShapeDtypeStruct((t, RET_V_W), _BF16),
                   jax.ShapeDtypeStruct(state_shape, _F32)),
        grid=(t // tm,),
        in_specs=[row(D_MODEL)] + rope_specs + [
            _const_spec((6, D_MODEL)), _const_spec((1, D_MODEL)), w_fwd_spec,
            _const_spec((RET_HEADS, 1, RET_DV)), _const_spec(state_shape)],
        out_specs=(row(RET_QK_W), row(RET_QK_W), row(RET_V_W), row(RET_V_W),
                   pl.BlockSpec(state_shape, lambda i: (0, 0, 0))),
        scratch_shapes=_decay_scratch() + [pltpu.VMEM((tm, RET_V_W), _F32)],
        compiler_params=_params(),
        name="retention_fwd",
    )(x, *(rope_tabs or ()), m, g, w_in, le, s0)


def _ret_bwd(x, q, k, v, yf, m, g, w_in, w_out, layer, le, s0, *, tm):
    t = x.shape[0]
    n = t // tm
    row = lambda w: pl.BlockSpec((tm, w), lambda i: (n - 1 - i, 0))
    state_shape = (RET_HEADS, RET_DK, RET_DV)
    gb_block = w_in.shape[2] // RET_V_W - 1
    w_gb_spec = pl.BlockSpec((None, D_MODEL, RET_V_W), lambda i: (layer, 0, gb_block),
                             pipeline_mode=pl.Buffered(1))
    return pl.pallas_call(
        functools.partial(_ret_bwd_kernel, tm=tm),
        out_shape=(jax.ShapeDtypeStruct((t, D_MODEL), _F32),
                   jax.ShapeDtypeStruct(state_shape, _F32)),
        grid=(n,),
        in_specs=[row(D_MODEL), row(RET_QK_W), row(RET_QK_W), row(RET_V_W), row(RET_V_W),
                  _const_spec((6, D_MODEL)), _const_spec((1, D_MODEL)),
                  w_gb_spec, _layer_spec(w_out, layer),
                  _const_spec((RET_HEADS, 1, RET_DV)), _const_spec(state_shape)],
        out_specs=(row(D_MODEL), pl.BlockSpec(state_shape, lambda i: (0, 0, 0))),
        scratch_shapes=_decay_scratch() + [pltpu.VMEM((tm, RET_V_W), _F32),
                                           pltpu.VMEM((tm, RET_V_W), _BF16)],
        compiler_params=_params(),
        name="retention_bwd",
    )(x, q, k, v, yf, m, g, w_in, w_out, le, s0)


def _rope_tables(n_tok):
    quarter = RET_DK // 4
    inv = ROPE_BASE ** (-jnp.arange(quarter, dtype=_F32) / quarter)
    ar = jnp.arange(n_tok // GRID_W, dtype=_F32)[:, None] * inv
    ac = jnp.arange(GRID_W, dtype=_F32)[:, None] * inv
    pair = lambda a, b: jnp.concatenate([a, b], axis=-1)
    return (pair(jnp.cos(ar), jnp.cos(ar)), pair(-jnp.sin(ar), jnp.sin(ar)),
            pair(jnp.cos(ac), jnp.cos(ac)), pair(-jnp.sin(ac), jnp.sin(ac)))


def kernel(x, c, ctx, c_ctx, mod_w, mod_b, norm1_g, norm2_g, conv_pw1_w, conv_pw1_b, conv_dw_w,
           conv_dw_b, conv_ln_g, conv_ln_b, conv_pw2_w, conv_pw2_b, ret_w_in, ret_log2_eps,
           ret_w_out, ffn_w_in, ffn_w_out, final_norm_g):
    assert x.shape[0] == 1 and x.shape[2] == D_MODEL
    n_tok = x.shape[1]
    n_ctx = ctx.shape[1]
    tm = 512
    assert n_tok % tm == 0 and n_ctx % RET_CHUNK == 0 and n_ctx % CONV_RB == 0
    xs = x[0]
    cs = ctx[0]
    vec = lambda a: a.reshape(1, -1)
    lvec = lambda a: a.reshape(a.shape[0], 1, -1)

    mods = _modulation(c, c_ctx, mod_w, mod_b)
    rope_tabs = _rope_tables(n_tok)
    zero_state = jnp.zeros((RET_HEADS, RET_DK, RET_DV), _F32)
    fg = vec(final_norm_g)
    conv_w = (conv_pw1_w.astype(_BF16), lvec(conv_pw1_b), conv_dw_w, lvec(conv_dw_b),
              lvec(conv_ln_g), lvec(conv_ln_b), conv_pw2_w.astype(_BF16), lvec(conv_pw2_b))
    w_ri = ret_w_in.astype(_BF16)
    w_ro = ret_w_out.astype(_BF16)
    f_in = ffn_w_in.astype(_BF16)
    f_out = ffn_w_out.astype(_BF16)

    for i in range(DEPTH):
        last = i == DEPTH - 1
        j = i // N_MIXERS
        m_lat = mods[i, 0].reshape(6, D_MODEL)
        m_ctx = mods[i, 1].reshape(6, D_MODEL)
        g1 = vec(norm1_g[i])
        g2 = vec(norm2_g[i])

        if i % N_MIXERS == 0:
            xs = _conv_mixer(xs, m_lat, g1, conv_w, j, tm=tm)
            if not last:
                cs = _conv_mixer(cs, m_ctx, g1, conv_w, j, tm=n_ctx)
        else:
            le = jnp.broadcast_to(ret_log2_eps[j][:, :, None, None], (2, RET_HEADS, 1, RET_DV))
            cq, ck, cv, cy, s_f = _ret_fwd(cs, None, m_ctx, g1, w_ri, j, le[0], zero_state,
                                           tm=n_ctx)
            cs_new, s_b = _ret_bwd(cs, cq, ck, cv, cy, m_ctx, g1, w_ri, w_ro, j, le[1],
                                   zero_state, tm=n_ctx)
            lq, lk, lv, ly, _ = _ret_fwd(xs, rope_tabs, m_lat, g1, w_ri, j, le[0], s_f, tm=tm)
            xs, _ = _ret_bwd(xs, lq, lk, lv, ly, m_lat, g1, w_ri, w_ro, j, le[1], s_b, tm=tm)
            if not last:
                cs = cs_new
        xs = _ffn(xs, m_lat, g2, f_in, f_out, i, fg, final=last, tm=tm)
        if not last:
            cs = _ffn(cs, m_ctx, g2, f_in, f_out, i, fg, final=False, tm=n_ctx)

    return xs[None]
```

```python
import functools

import jax
import jax.numpy as jnp
import numpy as np
from jax import lax
from jax.experimental import pallas as pl
from jax.experimental.pallas import tpu as pltpu

D_MODEL = 1024
DEPTH = 4
GRID_W = 64
N_MIXERS = 2
CONV_KERNEL = 31
CONV_PAD = CONV_KERNEL // 2
RET_HEADS = 4
RET_DK = D_MODEL // RET_HEADS
RET_DV = 2 * RET_DK
RET_QK_W = RET_HEADS * RET_DK
RET_V_W = RET_HEADS * RET_DV
RET_CHUNK = 256
ROPE_BASE = 10000.0
D_FF = 2816
NORM_EPS = 1e-6
LN_EPS = 1e-5

LANES = 128
HALO = 16
CONV_RB = 64
VMEM_LIMIT = 56 * 1024 * 1024

_BF16 = jnp.bfloat16
_F32 = jnp.float32


def _const_spec(shape):
    nd = len(shape)
    return pl.BlockSpec(shape, lambda i: (0,) * nd, pipeline_mode=pl.Buffered(1))


def _layer_spec(stacked, layer):
    return pl.BlockSpec((None,) + stacked.shape[1:], lambda i: (layer,) + (0,) * (stacked.ndim - 1),
                        pipeline_mode=pl.Buffered(1))


def _params(n_grid_axes=1):
    return pltpu.CompilerParams(
        dimension_semantics=("arbitrary",) * n_grid_axes,
        vmem_limit_bytes=VMEM_LIMIT)


def _modnorm(x, g, shift, scale):
    ms = jnp.mean(x * x, axis=-1, keepdims=True)
    return (x * lax.rsqrt(ms + NORM_EPS)) * (g * (1.0 + scale)) + shift


def _silu(v):
    return v * jax.nn.sigmoid(v)


def _dot(a, b):
    return jnp.dot(a, b, preferred_element_type=_F32)


MOD_TN = 3072


def _mod_kernel(cb_ref, w_ref, b_ref, o_ref):
    s_lat = _silu(cb_ref[0])
    s_ctx = _silu(cb_ref[1])
    o_ref[...] = jnp.zeros(o_ref.shape, _F32)
    for j in range(MOD_TN // LANES):
        cols = slice(j * LANES, (j + 1) * LANES)
        w = w_ref[0, :, cols]
        bias = b_ref[0, :, cols]
        o_ref[0, 0:1, cols] = jnp.sum(w * s_lat, axis=0, keepdims=True) + bias
        o_ref[0, 1:2, cols] = jnp.sum(w * s_ctx, axis=0, keepdims=True) + bias


def _modulation(c, c_ctx, mod_w, mod_b):
    cvec = jnp.stack([c[0], c_ctx])
    cb = jnp.broadcast_to(cvec[:, :, None], (2, D_MODEL, LANES))
    n_out = 6 * D_MODEL
    return pl.pallas_call(
        _mod_kernel,
        out_shape=jax.ShapeDtypeStruct((DEPTH, 8, n_out), _F32),
        grid=(DEPTH, n_out // MOD_TN),
        in_specs=[
            pl.BlockSpec((2, D_MODEL, LANES), lambda i, j: (0, 0, 0)),
            pl.BlockSpec((1, D_MODEL, MOD_TN), lambda i, j: (i, 0, j)),
            pl.BlockSpec((1, 1, MOD_TN), lambda i, j: (i, 0, j)),
        ],
        out_specs=pl.BlockSpec((1, 8, MOD_TN), lambda i, j: (i, 0, j)),
        compiler_params=_params(2),
        name="modulation",
    )(cb, mod_w, mod_b.reshape(DEPTH, 1, n_out))


FFN_SUB = 256


def _ffn_kernel(x_ref, m_ref, g_ref, win_ref, wout_ref, fg_ref, o_ref, *, final, sub):
    for r0 in range(0, x_ref.shape[0], sub):
        rows = slice(r0, r0 + sub)
        x = x_ref[rows, :]
        h = _modnorm(x, g_ref[...], m_ref[3:4, :], m_ref[4:5, :]).astype(_BF16)
        gt = _dot(h, win_ref[:, :D_FF])
        up = _dot(h, win_ref[:, D_FF:])
        a = (_silu(gt) * up).astype(_BF16)
        y = x + m_ref[5:6, :] * _dot(a, wout_ref[...])
        if final:
            ms = jnp.mean(y * y, axis=-1, keepdims=True)
            y = (y * lax.rsqrt(ms + NORM_EPS)) * fg_ref[...]
        o_ref[rows, :] = y


def _ffn(x, m, g, w_in, w_out, layer, fg, *, final, tm):
    t = x.shape[0]
    row = pl.BlockSpec((tm, D_MODEL), lambda i: (i, 0))
    return pl.pallas_call(
        functools.partial(_ffn_kernel, final=final, sub=min(tm, FFN_SUB)),
        out_shape=jax.ShapeDtypeStruct((t, D_MODEL), _F32),
        grid=(t // tm,),
        in_specs=[row, _const_spec((6, D_MODEL)), _const_spec((1, D_MODEL)),
                  _layer_spec(w_in, layer), _layer_spec(w_out, layer),
                  _const_spec((1, D_MODEL))],
        out_specs=row,
        compiler_params=_params(),
        name="ffn",
    )(x, m, g, w_in, w_out, fg)


CONV_SPAN = CONV_RB + 2 * HALO
CONV_SH = CONV_SPAN - 8


def _shift_matrix():
    s = np.zeros((7 * CONV_SH, 2 * CONV_SPAN), np.float32)
    for b in range(1, 8):
        for j in range(CONV_SH):
            s[(b - 1) * CONV_SH + j, j + b] = 1.0
            s[(b - 1) * CONV_SH + j, CONV_SPAN + j + b] = 1.0
    return jnp.asarray(s, _BF16)


def _conv_kernel(x_ref, xp_ref, xn_ref, m_ref, g_ref, pw1_ref, b1_ref, dw_ref, dwb_ref,
                 lng_ref, lnb_ref, pw2_ref, b2_ref, sh_ref, o_ref, u_scr, s_scr, c_scr,
                 *, tm, t_total):
    i = pl.program_id(0)
    x = x_ref[...]
    xa = jnp.concatenate([xp_ref[...], x, xn_ref[...]], axis=0)
    h = _modnorm(xa, g_ref[...], m_ref[0:1, :], m_ref[1:2, :]).astype(_BF16)
    n_rows = tm + 2 * HALO
    n_rb = tm // CONV_RB
    split = (n_rb // 2) * CONV_RB + 2 * HALO if n_rb > 1 else n_rows
    for r0, r1 in ((0, split), (split, n_rows)):
        if r0 == r1:
            continue
        z = _dot(h[r0:r1], pw1_ref[...]) + b1_ref[...]
        u = z[:, :D_MODEL] * jax.nn.sigmoid(z[:, D_MODEL:])
        r = i * tm - HALO + r0 + lax.broadcasted_iota(jnp.int32, (r1 - r0, 1), 0)
        u_scr[r0:r1, :] = jnp.where((r >= 0) & (r < t_total), u, 0.0)

    for rb in range(n_rb):
        base = rb * CONV_RB
        src = u_scr[base:base + CONV_SPAN, :]
        hi = src.astype(_BF16)
        lo = (src - hi.astype(_F32)).astype(_BF16)
        s_scr[...] = _dot(sh_ref[...], jnp.concatenate([hi, lo], axis=0))
        for lt in range(D_MODEL // LANES):
            cols = slice(lt * LANES, (lt + 1) * LANES)
            acc = jnp.broadcast_to(dwb_ref[:, cols], (CONV_RB, LANES))
            for b in range(8):
                for a in range(2 * HALO // 8):
                    k = 8 * a + b - (HALO - CONV_PAD)
                    if not 0 <= k < CONV_KERNEL:
                        continue
                    if b == 0:
                        rows = u_scr[base + 8 * a:base + 8 * a + CONV_RB, cols]
                    else:
                        off = (b - 1) * CONV_SH + 8 * a
                        rows = s_scr[off:off + CONV_RB, cols]
                    acc = acc + rows * dw_ref[k:k + 1, cols]
            c_scr[base:base + CONV_RB, cols] = acc

    acc = c_scr[...]
    mu = jnp.mean(acc, axis=-1, keepdims=True)
    d = acc - mu
    var = jnp.mean(d * d, axis=-1, keepdims=True)
    v = _silu(d * lax.rsqrt(var + LN_EPS) * lng_ref[...] + lnb_ref[...]).astype(_BF16)
    y = _dot(v, pw2_ref[...]) + b2_ref[...]
    o_ref[...] = x + m_ref[2:3, :] * y


def _conv_mixer(x, m, g, conv_w, layer, *, tm):
    pw1, b1, dw, dwb, lng, lnb, pw2, b2 = conv_w
    t = x.shape[0]
    hb = tm // HALO
    n_hb = t // HALO
    row = pl.BlockSpec((tm, D_MODEL), lambda i: (i, 0))
    prev = pl.BlockSpec((HALO, D_MODEL), lambda i: (jnp.maximum(i * hb - 1, 0), 0))
    nxt = pl.BlockSpec((HALO, D_MODEL), lambda i: (jnp.minimum((i + 1) * hb, n_hb - 1), 0))
    lspec = lambda a: _layer_spec(a, layer)
    shift = _shift_matrix()
    return pl.pallas_call(
        functools.partial(_conv_kernel, tm=tm, t_total=t),
        out_shape=jax.ShapeDtypeStruct((t, D_MODEL), _F32),
        grid=(t // tm,),
        in_specs=[row, prev, nxt, _const_spec((6, D_MODEL)), _const_spec((1, D_MODEL)),
                  lspec(pw1), lspec(b1), lspec(dw), lspec(dwb), lspec(lng), lspec(lnb),
                  lspec(pw2), lspec(b2), _const_spec(shift.shape)],
        out_specs=row,
        scratch_shapes=[pltpu.VMEM((tm + 2 * HALO, D_MODEL), _F32),
                        pltpu.VMEM((7 * CONV_SH, D_MODEL), _F32),
                        pltpu.VMEM((tm, D_MODEL), _F32)],
        compiler_params=_params(),
        name="conv_mixer",
    )(x, x, x, m, g, pw1, b1, dw, dwb, lng, lnb, pw2, b2, shift)


def _decay_tables(le_ref, dmat_scr, qd_scr, kd_scr, cd_scr, *, reverse):
    c = RET_CHUNK
    ri = lax.broadcasted_iota(jnp.int32, (c, c), 0)
    ci = lax.broadcasted_iota(jnp.int32, (c, c), 1)
    rel = (ci - ri) if reverse else (ri - ci)
    relf = jnp.maximum(rel, 0).astype(_F32)
    pos = lax.broadcasted_iota(jnp.int32, (c, RET_DV), 0).astype(_F32)
    posk = lax.broadcasted_iota(jnp.int32, (c, RET_DK), 0).astype(_F32)
    for hd in range(RET_HEADS):
        log_g = jnp.log(1.0 - jnp.exp2(le_ref[hd]))
        log_gk = jnp.log(1.0 - jnp.exp2(le_ref[hd, :, :RET_DK]))
        log_gc = jnp.log(1.0 - jnp.exp2(le_ref[hd, :, :c]))
        dmat_scr[hd] = jnp.where(rel >= 0, jnp.exp(relf * log_gc), 0.0)
        if reverse:
            qd_scr[hd] = jnp.exp((c - pos) * log_g)
            kd_scr[hd] = jnp.exp(posk * log_gk)
        else:
            qd_scr[hd] = jnp.exp((pos + 1.0) * log_g)
            kd_scr[hd] = jnp.exp((c - 1.0 - posk) * log_gk)
        cd_scr[hd] = jnp.exp(float(c) * log_g)


def _retention_chunk(q, k, v, hd, state_ref, dmat_scr, qd_scr, kd_scr, cd_scr):
    s = lax.dot_general(q, k, (((1,), (1,)), ((), ())), preferred_element_type=_F32)
    s = (s * dmat_scr[hd]).astype(_BF16)
    st = state_ref[hd]
    o = _dot(s, v) + _dot(q, st.astype(_BF16)) * qd_scr[hd]
    kd = (k.astype(_F32) * kd_scr[hd]).astype(_BF16)
    upd = lax.dot_general(kd, v, (((0,), (0,)), ((), ())), preferred_element_type=_F32)
    state_ref[hd] = st * cd_scr[hd] + upd
    return o


def _head_norm(o):
    return o * lax.rsqrt(jnp.mean(o * o, axis=-1, keepdims=True) + NORM_EPS)


def _ret_fwd_kernel(*refs, tm, rope):
    if rope:
        (x_ref, rcos_ref, rsin_ref, ccos_ref, csin_ref, m_ref, g_ref, w_ref, le_ref, s0_ref,
         q_ref, k_ref, v_ref, y_ref, state_ref,
         dmat_scr, qd_scr, kd_scr, cd_scr, gate_scr) = refs
    else:
        (x_ref, m_ref, g_ref, w_ref, le_ref, s0_ref,
         q_ref, k_ref, v_ref, y_ref, state_ref,
         dmat_scr, qd_scr, kd_scr, cd_scr, gate_scr) = refs

    @pl.when(pl.program_id(0) == 0)
    def _():
        _decay_tables(le_ref, dmat_scr, qd_scr, kd_scr, cd_scr, reverse=False)
        state_ref[...] = s0_ref[...]

    k_scale = RET_DK ** -0.5
    blocks_per_chunk = RET_CHUNK // GRID_W
    for j in range(tm // RET_CHUNK):
        crow = slice(j * RET_CHUNK, (j + 1) * RET_CHUNK)
        h = _modnorm(x_ref[crow, :], g_ref[...], m_ref[0:1, :], m_ref[1:2, :]).astype(_BF16)
        qf = _dot(h, w_ref[:, :RET_QK_W])
        kf = _dot(h, w_ref[:, RET_QK_W:2 * RET_QK_W])
        for b in range(blocks_per_chunk):
            mb = j * blocks_per_chunk + b
            rows = slice(mb * GRID_W, (mb + 1) * GRID_W)
            brow = slice(b * GRID_W, (b + 1) * GRID_W)
            if rope:
                row_cs = jnp.broadcast_to(rcos_ref[mb:mb + 1, :], (GRID_W, LANES))
                row_sn = jnp.broadcast_to(rsin_ref[mb:mb + 1, :], (GRID_W, LANES))
            for l in range(RET_QK_W // LANES):
                cols = slice(l * LANES, (l + 1) * LANES)
                qt = qf[brow, cols]
                kt = kf[brow, cols]
                if rope:
                    cs = row_cs if l % 2 == 0 else ccos_ref[...]
                    sn = row_sn if l % 2 == 0 else csin_ref[...]
                    qt = qt * cs + pltpu.roll(qt, LANES // 2, axis=1) * sn
                    kt = kt * cs + pltpu.roll(kt, LANES // 2, axis=1) * sn
                q_ref[rows, cols] = qt.astype(_BF16)
                k_ref[rows, cols] = (kt * k_scale).astype(_BF16)
        v_ref[crow, :] = _dot(h, w_ref[:, 2 * RET_QK_W:2 * RET_QK_W + RET_V_W]).astype(_BF16)
        gate_scr[crow, :] = _silu(_dot(h, w_ref[:, 2 * RET_QK_W + RET_V_W:]))

        for hd in range(RET_HEADS):
            qk_cols = slice(hd * RET_DK, (hd + 1) * RET_DK)
            v_cols = slice(hd * RET_DV, (hd + 1) * RET_DV)
            o = _retention_chunk(q_ref[crow, qk_cols], k_ref[crow, qk_cols], v_ref[crow, v_cols],
                                 hd, state_ref, dmat_scr, qd_scr, kd_scr, cd_scr)
            y_ref[crow, v_cols] = (gate_scr[crow, v_cols] * _head_norm(o)).astype(_BF16)


def _ret_bwd_kernel(x_ref, q_ref, k_ref, v_ref, yf_ref, m_ref, g_ref, wg_ref, wo_ref, le_ref,
                    s0_ref, o_ref, state_ref,
                    dmat_scr, qd_scr, kd_scr, cd_scr, gate_scr, y_scr, *, tm):
    @pl.when(pl.program_id(0) == 0)
    def _():
        _decay_tables(le_ref, dmat_scr, qd_scr, kd_scr, cd_scr, reverse=True)
        state_ref[...] = s0_ref[...]

    x = x_ref[...]
    h = _modnorm(x, g_ref[...], m_ref[0:1, :], m_ref[1:2, :]).astype(_BF16)
    gate_scr[...] = _silu(_dot(h, wg_ref[...]))

    for j in reversed(range(tm // RET_CHUNK)):
        rows = slice(j * RET_CHUNK, (j + 1) * RET_CHUNK)
        for hd in range(RET_HEADS):
            qk_cols = slice(hd * RET_DK, (hd + 1) * RET_DK)
            v_cols = slice(hd * RET_DV, (hd + 1) * RET_DV)
            o = _retention_chunk(q_ref[rows, qk_cols], k_ref[rows, qk_cols], v_ref[rows, v_cols],
                                 hd, state_ref, dmat_scr, qd_scr, kd_scr, cd_scr)
            y = yf_ref[rows, v_cols].astype(_F32) + gate_scr[rows, v_cols] * _head_norm(o)
            y_scr[rows, v_cols] = y.astype(_BF16)
    o_ref[...] = x + m_ref[2:3, :] * _dot(y_scr[...], wo_ref[...])


def _decay_scratch():
    return [pltpu.VMEM((RET_HEADS, RET_CHUNK, RET_CHUNK), _F32),
            pltpu.VMEM((RET_HEADS, RET_CHUNK, RET_DV), _F32),
            pltpu.VMEM((RET_HEADS, RET_CHUNK, RET_DK), _F32),
            pltpu.VMEM((RET_HEADS, 1, RET_DV), _F32)]


def _ret_fwd(x, rope_tabs, m, g, w_in, layer, le, s0, *, tm):
    t = x.shape[0]
    row = lambda w: pl.BlockSpec((tm, w), lambda i: (i, 0))
    state_shape = (RET_HEADS, RET_DK, RET_DV)
    rope = rope_tabs is not None
    rope_specs = []
    if rope:
        row_tab = pl.BlockSpec((tm // GRID_W, LANES), lambda i: (i, 0))
        rope_specs = [row_tab, row_tab, _const_spec((GRID_W, LANES)), _const_spec((GRID_W, LANES))]
    w_fwd_spec = pl.BlockSpec((None, D_MODEL, 2 * RET_QK_W + 2 * RET_V_W),
                              lambda i: (layer, 0, 0), pipeline_mode=pl.Buffered(1))
    return pl.pallas_call(
        functools.partial(_ret_fwd_kernel, tm=tm, rope=rope),
        out_shape=(jax.ShapeDtypeStruct((t, RET_QK_W), _BF16),
                   jax.ShapeDtypeStruct((t, RET_QK_W), _BF16),
                   jax.ShapeDtypeStruct((t, RET_V_W), _BF16),
                   jax.ShapeDtypeStruct((t, RET_V_W), _BF16),
                   jax.ShapeDtypeStruct(state_shape, _F32)),
        grid=(t // tm,),
        in_specs=[row(D_MODEL)] + rope_specs + [
            _const_spec((6, D_MODEL)), _const_spec((1, D_MODEL)), w_fwd_spec,
            _const_spec((RET_HEADS, 1, RET_DV)), _const_spec(state_shape)],
        out_specs=(row(RET_QK_W), row(RET_QK_W), row(RET_V_W), row(RET_V_W),
                   pl.BlockSpec(state_shape, lambda i: (0, 0, 0))),
        scratch_shapes=_decay_scratch() + [pltpu.VMEM((tm, RET_V_W), _F32)],
        compiler_params=_params(),
        name="retention_fwd",
    )(x, *(rope_tabs or ()), m, g, w_in, le, s0)


def _ret_bwd(x, q, k, v, yf, m, g, w_in, w_out, layer, le, s0, *, tm):
    t = x.shape[0]
    n = t // tm
    row = lambda w: pl.BlockSpec((tm, w), lambda i: (n - 1 - i, 0))
    state_shape = (RET_HEADS, RET_DK, RET_DV)
    gb_block = w_in.shape[2] // RET_V_W - 1
    w_gb_spec = pl.BlockSpec((None, D_MODEL, RET_V_W), lambda i: (layer, 0, gb_block),
                             pipeline_mode=pl.Buffered(1))
    return pl.pallas_call(
        functools.partial(_ret_bwd_kernel, tm=tm),
        out_shape=(jax.ShapeDtypeStruct((t, D_MODEL), _F32),
                   jax.ShapeDtypeStruct(state_shape, _F32)),
        grid=(n,),
        in_specs=[row(D_MODEL), row(RET_QK_W), row(RET_QK_W), row(RET_V_W), row(RET_V_W),
                  _const_spec((6, D_MODEL)), _const_spec((1, D_MODEL)),
                  w_gb_spec, _layer_spec(w_out, layer),
                  _const_spec((RET_HEADS, 1, RET_DV)), _const_spec(state_shape)],
        out_specs=(row(D_MODEL), pl.BlockSpec(state_shape, lambda i: (0, 0, 0))),
        scratch_shapes=_decay_scratch() + [pltpu.VMEM((tm, RET_V_W), _F32),
                                           pltpu.VMEM((tm, RET_V_W), _BF16)],
        compiler_params=_params(),
        name="retention_bwd",
    )(x, q, k, v, yf, m, g, w_in, w_out, le, s0)


def _rope_tables(n_tok):
    quarter = RET_DK // 4
    inv = ROPE_BASE ** (-jnp.arange(quarter, dtype=_F32) / quarter)
    ar = jnp.arange(n_tok // GRID_W, dtype=_F32)[:, None] * inv
    ac = jnp.arange(GRID_W, dtype=_F32)[:, None] * inv
    pair = lambda a, b: jnp.concatenate([a, b], axis=-1)
    return (pair(jnp.cos(ar), jnp.cos(ar)), pair(-jnp.sin(ar), jnp.sin(ar)),
            pair(jnp.cos(ac), jnp.cos(ac)), pair(-jnp.sin(ac), jnp.sin(ac)))


def kernel(x, c, ctx, c_ctx, mod_w, mod_b, norm1_g, norm2_g, conv_pw1_w, conv_pw1_b, conv_dw_w,
           conv_dw_b, conv_ln_g, conv_ln_b, conv_pw2_w, conv_pw2_b, ret_w_in, ret_log2_eps,
           ret_w_out, ffn_w_in, ffn_w_out, final_norm_g):
    assert x.shape[0] == 1 and x.shape[2] == D_MODEL
    n_tok = x.shape[1]
    n_ctx = ctx.shape[1]
    tm = 512
    assert n_tok % tm == 0 and n_ctx % RET_CHUNK == 0 and n_ctx % CONV_RB == 0
    xs = x[0]
    cs = ctx[0]
    vec = lambda a: a.reshape(1, -1)
    lvec = lambda a: a.reshape(a.shape[0], 1, -1)

    mods = _modulation(c, c_ctx, mod_w, mod_b)
    rope_tabs = _rope_tables(n_tok)
    zero_state = jnp.zeros((RET_HEADS, RET_DK, RET_DV), _F32)
    fg = vec(final_norm_g)
    conv_w = (conv_pw1_w.astype(_BF16), lvec(conv_pw1_b), conv_dw_w, lvec(conv_dw_b),
              lvec(conv_ln_g), lvec(conv_ln_b), conv_pw2_w.astype(_BF16), lvec(conv_pw2_b))
    w_ri = ret_w_in.astype(_BF16)
    w_ro = ret_w_out.astype(_BF16)
    f_in = ffn_w_in.astype(_BF16)
    f_out = ffn_w_out.astype(_BF16)

    for i in range(DEPTH):
        last = i == DEPTH - 1
        j = i // N_MIXERS
        m_lat = mods[i, 0].reshape(6, D_MODEL)
        m_ctx = mods[i, 1].reshape(6, D_MODEL)
        g1 = vec(norm1_g[i])
        g2 = vec(norm2_g[i])

        if i % N_MIXERS == 0:
            xs = _conv_mixer(xs, m_lat, g1, conv_w, j, tm=tm)
            if not last:
                cs = _conv_mixer(cs, m_ctx, g1, conv_w, j, tm=n_ctx)
        else:
            le = jnp.broadcast_to(ret_log2_eps[j][:, :, None, None], (2, RET_HEADS, 1, RET_DV))
            cq, ck, cv, cy, s_f = _ret_fwd(cs, None, m_ctx, g1, w_ri, j, le[0], zero_state,
                                           tm=n_ctx)
            cs_new, s_b = _ret_bwd(cs, cq, ck, cv, cy, m_ctx, g1, w_ri, w_ro, j, le[1],
                                   zero_state, tm=n_ctx)
            lq, lk, lv, ly, _ = _ret_fwd(xs, rope_tabs, m_lat, g1, w_ri, j, le[0], s_f, tm=tm)
            xs, _ = _ret_bwd(xs, lq, lk, lv, ly, m_lat, g1, w_ri, w_ro, j, le[1], s_b, tm=tm)
            if not last:
                cs = cs_new
        xs = _ffn(xs, m_lat, g2, f_in, f_out, i, fg, final=last, tm=tm)
        if not last:
            cs = _ffn(cs, m_ctx, g2, f_in, f_out, i, fg, final=False, tm=n_ctx)

    return xs[None]
```

```python
import functools

import jax
import jax.numpy as jnp
import numpy as np
from jax import lax
from jax.experimental import pallas as pl
from jax.experimental.pallas import tpu as pltpu

D_MODEL = 1024
DEPTH = 4
GRID_W = 64
N_MIXERS = 2
CONV_KERNEL = 31
CONV_PAD = CONV_KERNEL // 2
RET_HEADS = 4
RET_DK = D_MODEL // RET_HEADS
RET_DV = 2 * RET_DK
RET_QK_W = RET_HEADS * RET_DK
RET_V_W = RET_HEADS * RET_DV
RET_CHUNK = 256
ROPE_BASE = 10000.0
D_FF = 2816
NORM_EPS = 1e-6
LN_EPS = 1e-5

LANES = 128
HALO = 16
CONV_RB = 64
VMEM_LIMIT = 56 * 1024 * 1024

_BF16 = jnp.bfloat16
_F32 = jnp.float32


def _const_spec(shape):
    nd = len(shape)
    return pl.BlockSpec(shape, lambda i: (0,) * nd, pipeline_mode=pl.Buffered(1))


def _layer_spec(stacked, layer):
    return pl.BlockSpec((None,) + stacked.shape[1:], lambda i: (layer,) + (0,) * (stacked.ndim - 1),
                        pipeline_mode=pl.Buffered(1))


def _ctx_out_spec(width, n_ctx):
    return pl.BlockSpec((n_ctx, width), lambda i: (0, 0))


def _params(n_grid_axes=1):
    return pltpu.CompilerParams(
        dimension_semantics=("arbitrary",) * n_grid_axes,
        vmem_limit_bytes=VMEM_LIMIT)


def _modnorm(x, g, shift, scale):
    ms = jnp.mean(x * x, axis=-1, keepdims=True)
    return (x * lax.rsqrt(ms + NORM_EPS)) * (g * (1.0 + scale)) + shift


def _silu(v):
    return v * jax.nn.sigmoid(v)


def _dot(a, b):
    return jnp.dot(a, b, preferred_element_type=_F32)


def _on_context_then_latent(ctx_fn, lat_fn):
    step = pl.program_id(0)
    pl.when(step == 0)(ctx_fn)
    pl.when(step > 0)(lat_fn)


MOD_TN = 3072


def _mod_kernel(cb_ref, w_ref, b_ref, o_ref):
    s_lat = _silu(cb_ref[0])
    s_ctx = _silu(cb_ref[1])
    o_ref[...] = jnp.zeros(o_ref.shape, _F32)
    for j in range(MOD_TN // LANES):
        cols = slice(j * LANES, (j + 1) * LANES)
        w = w_ref[0, :, cols]
        bias = b_ref[0, :, cols]
        o_ref[0, 0:1, cols] = jnp.sum(w * s_lat, axis=0, keepdims=True) + bias
        o_ref[0, 1:2, cols] = jnp.sum(w * s_ctx, axis=0, keepdims=True) + bias


def _modulation(c, c_ctx, mod_w, mod_b):
    cvec = jnp.stack([c[0], c_ctx])
    cb = jnp.broadcast_to(cvec[:, :, None], (2, D_MODEL, LANES))
    n_out = 6 * D_MODEL
    return pl.pallas_call(
        _mod_kernel,
        out_shape=jax.ShapeDtypeStruct((DEPTH, 8, n_out), _F32),
        grid=(DEPTH, n_out // MOD_TN),
        in_specs=[
            pl.BlockSpec((2, D_MODEL, LANES), lambda i, j: (0, 0, 0)),
            pl.BlockSpec((1, D_MODEL, MOD_TN), lambda i, j: (i, 0, j)),
            pl.BlockSpec((1, 1, MOD_TN), lambda i, j: (i, 0, j)),
        ],
        out_specs=pl.BlockSpec((1, 8, MOD_TN), lambda i, j: (i, 0, j)),
        compiler_params=_params(2),
        name="modulation",
    )(cb, mod_w, mod_b.reshape(DEPTH, 1, n_out))


FFN_SUB = 256


def _ffn_rows(x_ref, m_ref, o_ref, g_ref, win_ref, wout_ref, fg_ref, *, final):
    n_rows = x_ref.shape[0]
    sub = min(n_rows, FFN_SUB)
    for r0 in range(0, n_rows, sub):
        rows = slice(r0, r0 + sub)
        x = x_ref[rows, :]
        h = _modnorm(x, g_ref[...], m_ref[3:4, :], m_ref[4:5, :]).astype(_BF16)
        gt = _dot(h, win_ref[:, :D_FF])
        up = _dot(h, win_ref[:, D_FF:])
        a = (_silu(gt) * up).astype(_BF16)
        y = x + m_ref[5:6, :] * _dot(a, wout_ref[...])
        if final:
            ms = jnp.mean(y * y, axis=-1, keepdims=True)
            y = (y * lax.rsqrt(ms + NORM_EPS)) * fg_ref[...]
        o_ref[rows, :] = y


def _ffn_kernel(*refs, final, with_ctx):
    if with_ctx:
        x_ref, cx_ref, m_ref, cm_ref, g_ref, win_ref, wout_ref, fg_ref, o_ref, co_ref = refs
        run = functools.partial(_ffn_rows, g_ref=g_ref, win_ref=win_ref, wout_ref=wout_ref,
                                fg_ref=fg_ref, final=final)
        _on_context_then_latent(functools.partial(run, cx_ref, cm_ref, co_ref),
                                functools.partial(run, x_ref, m_ref, o_ref))
    else:
        x_ref, m_ref, g_ref, win_ref, wout_ref, fg_ref, o_ref = refs
        _ffn_rows(x_ref, m_ref, o_ref, g_ref, win_ref, wout_ref, fg_ref, final=final)


def _ffn(x, m, g, w_in, w_out, layer, fg, *, final, tm, ctx=None):
    t = x.shape[0]
    first = 0 if ctx is None else 1
    row = pl.BlockSpec((tm, D_MODEL), lambda i: (jnp.maximum(i - first, 0), 0))
    mod = _const_spec((6, D_MODEL))
    weights = [_const_spec((1, D_MODEL)), _layer_spec(w_in, layer), _layer_spec(w_out, layer),
               _const_spec((1, D_MODEL))]
    lat_sds = jax.ShapeDtypeStruct((t, D_MODEL), _F32)
    if ctx is None:
        in_specs, args = [row, mod] + weights, (x, m, g, w_in, w_out, fg)
        out_shape, out_specs = lat_sds, row
    else:
        cx, cm = ctx
        n_ctx = cx.shape[0]
        in_specs = [row, _const_spec((n_ctx, D_MODEL)), mod, mod] + weights
        args = (x, cx, m, cm, g, w_in, w_out, fg)
        out_shape = (lat_sds, jax.ShapeDtypeStruct((n_ctx, D_MODEL), _F32))
        out_specs = (row, _ctx_out_spec(D_MODEL, n_ctx))
    return pl.pallas_call(
        functools.partial(_ffn_kernel, final=final, with_ctx=ctx is not None),
        out_shape=out_shape,
        grid=(t // tm + first,),
        in_specs=in_specs,
        out_specs=out_specs,
        compiler_params=_params(),
        name="ffn",
    )(*args)


CONV_SPAN = CONV_RB + 2 * HALO
CONV_SH = CONV_SPAN - 8


def _shift_matrix():
    s = np.zeros((7 * CONV_SH, 2 * CONV_SPAN), np.float32)
    for b in range(1, 8):
        for j in range(CONV_SH):
            s[(b - 1) * CONV_SH + j, j + b] = 1.0
            s[(b - 1) * CONV_SH + j, CONV_SPAN + j + b] = 1.0
    return jnp.asarray(s, _BF16)


def _conv_rows(tile, x_ref, xp_ref, xn_ref, m_ref, o_ref, g_ref, pw1_ref, b1_ref, dw_ref, dwb_ref,
               lng_ref, lnb_ref, pw2_ref, b2_ref, sh_ref, u_scr, s_scr, c_scr, *, t_total):
    tm = x_ref.shape[0]
    x = x_ref[...]
    xa = jnp.concatenate([xp_ref[...], x, xn_ref[...]], axis=0)
    h = _modnorm(xa, g_ref[...], m_ref[0:1, :], m_ref[1:2, :]).astype(_BF16)
    n_rows = tm + 2 * HALO
    n_rb = tm // CONV_RB
    split = (n_rb // 2) * CONV_RB + 2 * HALO if n_rb > 1 else n_rows
    for r0, r1 in ((0, split), (split, n_rows)):
        if r0 == r1:
            continue
        z = _dot(h[r0:r1], pw1_ref[...]) + b1_ref[...]
        u = z[:, :D_MODEL] * jax.nn.sigmoid(z[:, D_MODEL:])
        r = tile * tm - HALO + r0 + lax.broadcasted_iota(jnp.int32, (r1 - r0, 1), 0)
        u_scr[r0:r1, :] = jnp.where((r >= 0) & (r < t_total), u, 0.0)

    for rb in range(n_rb):
        base = rb * CONV_RB
        src = u_scr[base:base + CONV_SPAN, :]
        hi = src.astype(_BF16)
        lo = (src - hi.astype(_F32)).astype(_BF16)
        s_scr[...] = _dot(sh_ref[...], jnp.concatenate([hi, lo], axis=0))
        for lt in range(D_MODEL // LANES):
            cols = slice(lt * LANES, (lt + 1) * LANES)
            acc = jnp.broadcast_to(dwb_ref[:, cols], (CONV_RB, LANES))
            for b in range(8):
                for a in range(2 * HALO // 8):
                    k = 8 * a + b - (HALO - CONV_PAD)
                    if not 0 <= k < CONV_KERNEL:
                        continue
                    if b == 0:
                        rows = u_scr[base + 8 * a:base + 8 * a + CONV_RB, cols]
                    else:
                        off = (b - 1) * CONV_SH + 8 * a
                        rows = s_scr[off:off + CONV_RB, cols]
                    acc = acc + rows * dw_ref[k:k + 1, cols]
            c_scr[base:base + CONV_RB, cols] = acc

    acc = c_scr[0:tm, :]
    mu = jnp.mean(acc, axis=-1, keepdims=True)
    d = acc - mu
    var = jnp.mean(d * d, axis=-1, keepdims=True)
    v = _silu(d * lax.rsqrt(var + LN_EPS) * lng_ref[...] + lnb_ref[...]).astype(_BF16)
    y = _dot(v, pw2_ref[...]) + b2_ref[...]
    o_ref[...] = x + m_ref[2:3, :] * y


def _conv_kernel(x_ref, xp_ref, xn_ref, cx_ref, cxp_ref, cxn_ref, m_ref, cm_ref, g_ref,
                 pw1_ref, b1_ref, dw_ref, dwb_ref, lng_ref, lnb_ref, pw2_ref, b2_ref, sh_ref,
                 o_ref, co_ref, u_scr, s_scr, c_scr, *, t_total):
    shared = (g_ref, pw1_ref, b1_ref, dw_ref, dwb_ref, lng_ref, lnb_ref, pw2_ref, b2_ref, sh_ref,
              u_scr, s_scr, c_scr)
    latent_tile = pl.program_id(0) - 1
    _on_context_then_latent(
        lambda: _conv_rows(0, cx_ref, cxp_ref, cxn_ref, cm_ref, co_ref, *shared,
                           t_total=cx_ref.shape[0]),
        lambda: _conv_rows(latent_tile, x_ref, xp_ref, xn_ref, m_ref, o_ref, *shared,
                           t_total=t_total))


def _conv_mixer(x, cx, m, cm, g, conv_w, layer, *, tm):
    pw1, b1, dw, dwb, lng, lnb, pw2, b2 = conv_w
    t, n_ctx = x.shape[0], cx.shape[0]
    assert n_ctx <= tm
    hb = tm // HALO
    n_hb = t // HALO
    tile = lambda i: jnp.maximum(i - 1, 0)
    row = pl.BlockSpec((tm, D_MODEL), lambda i: (tile(i), 0))
    prev = pl.BlockSpec((HALO, D_MODEL), lambda i: (jnp.maximum(tile(i) * hb - 1, 0), 0))
    nxt = pl.BlockSpec((HALO, D_MODEL), lambda i: (jnp.minimum((tile(i) + 1) * hb, n_hb - 1), 0))
    lspec = lambda a: _layer_spec(a, layer)
    mod = _const_spec((6, D_MODEL))
    shift = _shift_matrix()
    return pl.pallas_call(
        functools.partial(_conv_kernel, t_total=t),
        out_shape=(jax.ShapeDtypeStruct((t, D_MODEL), _F32),
                   jax.ShapeDtypeStruct((n_ctx, D_MODEL), _F32)),
        grid=(t // tm + 1,),
        in_specs=[row, prev, nxt, _const_spec((n_ctx, D_MODEL)), _const_spec((HALO, D_MODEL)),
                  _const_spec((HALO, D_MODEL)), mod, mod, _const_spec((1, D_MODEL)),
                  lspec(pw1), lspec(b1), lspec(dw), lspec(dwb), lspec(lng), lspec(lnb),
                  lspec(pw2), lspec(b2), _const_spec(shift.shape)],
        out_specs=(row, _ctx_out_spec(D_MODEL, n_ctx)),
        scratch_shapes=[pltpu.VMEM((tm + 2 * HALO, D_MODEL), _F32),
                        pltpu.VMEM((7 * CONV_SH, D_MODEL), _F32),
                        pltpu.VMEM((tm, D_MODEL), _F32)],
        compiler_params=_params(),
        name="conv_mixer",
    )(x, x, x, cx, cx, cx, m, cm, g, pw1, b1, dw, dwb, lng, lnb, pw2, b2, shift)


def _decay_tables(le_ref, dmat_scr, qd_scr, kd_scr, cd_scr, *, reverse):
    c = RET_CHUNK
    ri = lax.broadcasted_iota(jnp.int32, (c, c), 0)
    ci = lax.broadcasted_iota(jnp.int32, (c, c), 1)
    rel = (ci - ri) if reverse else (ri - ci)
    relf = jnp.maximum(rel, 0).astype(_F32)
    pos = lax.broadcasted_iota(jnp.int32, (c, RET_DV), 0).astype(_F32)
    posk = lax.broadcasted_iota(jnp.int32, (c, RET_DK), 0).astype(_F32)
    for hd in range(RET_HEADS):
        log_g = jnp.log(1.0 - jnp.exp2(le_ref[hd]))
        log_gk = jnp.log(1.0 - jnp.exp2(le_ref[hd, :, :RET_DK]))
        log_gc = jnp.log(1.0 - jnp.exp2(le_ref[hd, :, :c]))
        dmat_scr[hd] = jnp.where(rel >= 0, jnp.exp(relf * log_gc), 0.0)
        if reverse:
            qd_scr[hd] = jnp.exp((c - pos) * log_g)
            kd_scr[hd] = jnp.exp(posk * log_gk)
        else:
            qd_scr[hd] = jnp.exp((pos + 1.0) * log_g)
            kd_scr[hd] = jnp.exp((c - 1.0 - posk) * log_gk)
        cd_scr[hd] = jnp.exp(float(c) * log_g)


def _retention_chunk(q, k, v, hd, state_scr, dmat_scr, qd_scr, kd_scr, cd_scr):
    s = lax.dot_general(q, k, (((1,), (1,)), ((), ())), preferred_element_type=_F32)
    s = (s * dmat_scr[hd]).astype(_BF16)
    st = state_scr[hd]
    o = _dot(s, v) + _dot(q, st.astype(_BF16)) * qd_scr[hd]
    kd = (k.astype(_F32) * kd_scr[hd]).astype(_BF16)
    upd = lax.dot_general(kd, v, (((0,), (0,)), ((), ())), preferred_element_type=_F32)
    state_scr[hd] = st * cd_scr[hd] + upd
    return o


def _head_norm(o):
    return o * lax.rsqrt(jnp.mean(o * o, axis=-1, keepdims=True) + NORM_EPS)


def _ret_fwd_rows(x_ref, rope_refs, m_ref, q_ref, k_ref, v_ref, y_ref, g_ref, w_ref,
                  state_scr, dmat_scr, qd_scr, kd_scr, cd_scr, gate_scr):
    k_scale = RET_DK ** -0.5
    blocks_per_chunk = RET_CHUNK // GRID_W
    for j in range(x_ref.shape[0] // RET_CHUNK):
        crow = slice(j * RET_CHUNK, (j + 1) * RET_CHUNK)
        h = _modnorm(x_ref[crow, :], g_ref[...], m_ref[0:1, :], m_ref[1:2, :]).astype(_BF16)
        qf = _dot(h, w_ref[:, :RET_QK_W])
        kf = _dot(h, w_ref[:, RET_QK_W:2 * RET_QK_W])
        for b in range(blocks_per_chunk):
            mb = j * blocks_per_chunk + b
            rows = slice(mb * GRID_W, (mb + 1) * GRID_W)
            brow = slice(b * GRID_W, (b + 1) * GRID_W)
            if rope_refs is not None:
                rcos_ref, rsin_ref, ccos_ref, csin_ref = rope_refs
                row_cs = jnp.broadcast_to(rcos_ref[mb:mb + 1, :], (GRID_W, LANES))
                row_sn = jnp.broadcast_to(rsin_ref[mb:mb + 1, :], (GRID_W, LANES))
            for l in range(RET_QK_W // LANES):
                cols = slice(l * LANES, (l + 1) * LANES)
                qt = qf[brow, cols]
                kt = kf[brow, cols]
                if rope_refs is not None:
                    cs = row_cs if l % 2 == 0 else ccos_ref[...]
                    sn = row_sn if l % 2 == 0 else csin_ref[...]
                    qt = qt * cs + pltpu.roll(qt, LANES // 2, axis=1) * sn
                    kt = kt * cs + pltpu.roll(kt, LANES // 2, axis=1) * sn
                q_ref[rows, cols] = qt.astype(_BF16)
                k_ref[rows, cols] = (kt * k_scale).astype(_BF16)
        v_ref[crow, :] = _dot(h, w_ref[:, 2 * RET_QK_W:2 * RET_QK_W + RET_V_W]).astype(_BF16)
        gate_scr[crow, :] = _silu(_dot(h, w_ref[:, 2 * RET_QK_W + RET_V_W:]))

        for hd in range(RET_HEADS):
            qk_cols = slice(hd * RET_DK, (hd + 1) * RET_DK)
            v_cols = slice(hd * RET_DV, (hd + 1) * RET_DV)
            o = _retention_chunk(q_ref[crow, qk_cols], k_ref[crow, qk_cols], v_ref[crow, v_cols],
                                 hd, state_scr, dmat_scr, qd_scr, kd_scr, cd_scr)
            y_ref[crow, v_cols] = (gate_scr[crow, v_cols] * _head_norm(o)).astype(_BF16)


def _ret_fwd_kernel(x_ref, cx_ref, rcos_ref, rsin_ref, ccos_ref, csin_ref, m_ref, cm_ref, g_ref,
                    w_ref, le_ref, q_ref, k_ref, v_ref, y_ref, cq_ref, ck_ref, cv_ref, cy_ref,
                    state_scr, dmat_scr, qd_scr, kd_scr, cd_scr, gate_scr):
    shared = (g_ref, w_ref, state_scr, dmat_scr, qd_scr, kd_scr, cd_scr, gate_scr)

    def context():
        _decay_tables(le_ref, dmat_scr, qd_scr, kd_scr, cd_scr, reverse=False)
        state_scr[...] = jnp.zeros(state_scr.shape, _F32)
        _ret_fwd_rows(cx_ref, None, cm_ref, cq_ref, ck_ref, cv_ref, cy_ref, *shared)

    def latent():
        _ret_fwd_rows(x_ref, (rcos_ref, rsin_ref, ccos_ref, csin_ref), m_ref,
                      q_ref, k_ref, v_ref, y_ref, *shared)

    _on_context_then_latent(context, latent)


def _ret_bwd_rows(x_ref, q_ref, k_ref, v_ref, yf_ref, m_ref, o_ref, g_ref, wg_ref, wo_ref,
                  state_scr, dmat_scr, qd_scr, kd_scr, cd_scr, gate_scr, y_scr):
    tm = x_ref.shape[0]
    x = x_ref[...]
    h = _modnorm(x, g_ref[...], m_ref[0:1, :], m_ref[1:2, :]).astype(_BF16)
    gate_scr[0:tm, :] = _silu(_dot(h, wg_ref[...]))

    for j in reversed(range(tm // RET_CHUNK)):
        rows = slice(j * RET_CHUNK, (j + 1) * RET_CHUNK)
        for hd in range(RET_HEADS):
            qk_cols = slice(hd * RET_DK, (hd + 1) * RET_DK)
            v_cols = slice(hd * RET_DV, (hd + 1) * RET_DV)
            o = _retention_chunk(q_ref[rows, qk_cols], k_ref[rows, qk_cols], v_ref[rows, v_cols],
                                 hd, state_scr, dmat_scr, qd_scr, kd_scr, cd_scr)
            y = yf_ref[rows, v_cols].astype(_F32) + gate_scr[rows, v_cols] * _head_norm(o)
            y_scr[rows, v_cols] = y.astype(_BF16)
    o_ref[...] = x + m_ref[2:3, :] * _dot(y_scr[0:tm, :], wo_ref[...])


def _ret_bwd_kernel(x_ref, q_ref, k_ref, v_ref, yf_ref, cx_ref, cq_ref, ck_ref, cv_ref, cyf_ref,
                    m_ref, cm_ref, g_ref, wg_ref, wo_ref, le_ref, o_ref, co_ref,
                    state_scr, dmat_scr, qd_scr, kd_scr, cd_scr, gate_scr, y_scr):
    shared = (g_ref, wg_ref, wo_ref, state_scr, dmat_scr, qd_scr, kd_scr, cd_scr, gate_scr, y_scr)

    def context():
        _decay_tables(le_ref, dmat_scr, qd_scr, kd_scr, cd_scr, reverse=True)
        state_scr[...] = jnp.zeros(state_scr.shape, _F32)
        _ret_bwd_rows(cx_ref, cq_ref, ck_ref, cv_ref, cyf_ref, cm_ref, co_ref, *shared)

    def latent():
        _ret_bwd_rows(x_ref, q_ref, k_ref, v_ref, yf_ref, m_ref, o_ref, *shared)

    _on_context_then_latent(context, latent)


def _scan_scratch():
    return [pltpu.VMEM((RET_HEADS, RET_DK, RET_DV), _F32),
            pltpu.VMEM((RET_HEADS, RET_CHUNK, RET_CHUNK), _F32),
            pltpu.VMEM((RET_HEADS, RET_CHUNK, RET_DV), _F32),
            pltpu.VMEM((RET_HEADS, RET_CHUNK, RET_DK), _F32),
            pltpu.VMEM((RET_HEADS, 1, RET_DV), _F32)]


def _qkvy_shapes(n_rows):
    return tuple(jax.ShapeDtypeStruct((n_rows, w), _BF16)
                 for w in (RET_QK_W, RET_QK_W, RET_V_W, RET_V_W))


_QKVY_WIDTHS = (RET_QK_W, RET_QK_W, RET_V_W, RET_V_W)


def _ret_fwd(x, cx, rope_tabs, m, cm, g, w_in, layer, le, *, tm):
    t, n_ctx = x.shape[0], cx.shape[0]
    assert n_ctx <= tm
    tile = lambda i: jnp.maximum(i - 1, 0)
    row = lambda w: pl.BlockSpec((tm, w), lambda i: (tile(i), 0))
    row_tab = pl.BlockSpec((tm // GRID_W, LANES), lambda i: (tile(i), 0))
    col_tab = _const_spec((GRID_W, LANES))
    mod = _const_spec((6, D_MODEL))
    w_fwd_spec = pl.BlockSpec((None, D_MODEL, 2 * RET_QK_W + 2 * RET_V_W),
                              lambda i: (layer, 0, 0), pipeline_mode=pl.Buffered(1))
    return pl.pallas_call(
        _ret_fwd_kernel,
        out_shape=_qkvy_shapes(t) + _qkvy_shapes(n_ctx),
        grid=(t // tm + 1,),
        in_specs=[row(D_MODEL), _const_spec((n_ctx, D_MODEL)), row_tab, row_tab, col_tab, col_tab,
                  mod, mod, _const_spec((1, D_MODEL)), w_fwd_spec,
                  _const_spec((RET_HEADS, 1, RET_DV))],
        out_specs=tuple(row(w) for w in _QKVY_WIDTHS)
        + tuple(_ctx_out_spec(w, n_ctx) for w in _QKVY_WIDTHS),
        scratch_shapes=_scan_scratch() + [pltpu.VMEM((tm, RET_V_W), _F32)],
        compiler_params=_params(),
        name="retention_fwd",
    )(x, cx, *rope_tabs, m, cm, g, w_in, le)


def _ret_bwd(x, qkvy, cx, cqkvy, m, cm, g, w_in, w_out, layer, le, *, tm):
    t, n_ctx = x.shape[0], cx.shape[0]
    n = t // tm
    assert n_ctx <= tm
    tile = lambda i: jnp.minimum(n - i, n - 1)
    row = lambda w: pl.BlockSpec((tm, w), lambda i: (tile(i), 0))
    mod = _const_spec((6, D_MODEL))
    gb_block = w_in.shape[2] // RET_V_W - 1
    w_gb_spec = pl.BlockSpec((None, D_MODEL, RET_V_W), lambda i: (layer, 0, gb_block),
                             pipeline_mode=pl.Buffered(1))
    return pl.pallas_call(
        _ret_bwd_kernel,
        out_shape=(jax.ShapeDtypeStruct((t, D_MODEL), _F32),
                   jax.ShapeDtypeStruct((n_ctx, D_MODEL), _F32)),
        grid=(n + 1,),
        in_specs=[row(D_MODEL)] + [row(w) for w in _QKVY_WIDTHS]
        + [_const_spec((n_ctx, D_MODEL))] + [_const_spec((n_ctx, w)) for w in _QKVY_WIDTHS]
        + [mod, mod, _const_spec((1, D_MODEL)), w_gb_spec, _layer_spec(w_out, layer),
           _const_spec((RET_HEADS, 1, RET_DV))],
        out_specs=(row(D_MODEL), _ctx_out_spec(D_MODEL, n_ctx)),
        scratch_shapes=_scan_scratch() + [pltpu.VMEM((tm, RET_V_W), _F32),
                                          pltpu.VMEM((tm, RET_V_W), _BF16)],
        compiler_params=_params(),
        name="retention_bwd",
    )(x, *qkvy, cx, *cqkvy, m, cm, g, w_in, w_out, le)


def _rope_tables(n_tok):
    quarter = RET_DK // 4
    inv = ROPE_BASE ** (-jnp.arange(quarter, dtype=_F32) / quarter)
    ar = jnp.arange(n_tok // GRID_W, dtype=_F32)[:, None] * inv
    ac = jnp.arange(GRID_W, dtype=_F32)[:, None] * inv
    pair = lambda a, b: jnp.concatenate([a, b], axis=-1)
    return (pair(jnp.cos(ar), jnp.cos(ar)), pair(-jnp.sin(ar), jnp.sin(ar)),
            pair(jnp.cos(ac), jnp.cos(ac)), pair(-jnp.sin(ac), jnp.sin(ac)))


def kernel(x, c, ctx, c_ctx, mod_w, mod_b, norm1_g, norm2_g, conv_pw1_w, conv_pw1_b, conv_dw_w,
           conv_dw_b, conv_ln_g, conv_ln_b, conv_pw2_w, conv_pw2_b, ret_w_in, ret_log2_eps,
           ret_w_out, ffn_w_in, ffn_w_out, final_norm_g):
    assert x.shape[0] == 1 and x.shape[2] == D_MODEL
    n_tok = x.shape[1]
    n_ctx = ctx.shape[1]
    tm = 512
    assert n_tok % tm == 0 and n_ctx % RET_CHUNK == 0 and n_ctx % CONV_RB == 0
    xs = x[0]
    cs = ctx[0]
    vec = lambda a: a.reshape(1, -1)
    lvec = lambda a: a.reshape(a.shape[0], 1, -1)

    mods = _modulation(c, c_ctx, mod_w, mod_b)
    rope_tabs = _rope_tables(n_tok)
    fg = vec(final_norm_g)
    conv_w = (conv_pw1_w.astype(_BF16), lvec(conv_pw1_b), conv_dw_w, lvec(conv_dw_b),
              lvec(conv_ln_g), lvec(conv_ln_b), conv_pw2_w.astype(_BF16), lvec(conv_pw2_b))
    w_ri = ret_w_in.astype(_BF16)
    w_ro = ret_w_out.astype(_BF16)
    f_in = ffn_w_in.astype(_BF16)
    f_out = ffn_w_out.astype(_BF16)

    for i in range(DEPTH):
        last = i == DEPTH - 1
        j = i // N_MIXERS
        m_lat = mods[i, 0].reshape(6, D_MODEL)
        m_ctx = mods[i, 1].reshape(6, D_MODEL)
        g1 = vec(norm1_g[i])
        g2 = vec(norm2_g[i])

        if i % N_MIXERS == 0:
            xs, cs = _conv_mixer(xs, cs, m_lat, m_ctx, g1, conv_w, j, tm=tm)
        else:
            le = jnp.broadcast_to(ret_log2_eps[j][:, :, None, None], (2, RET_HEADS, 1, RET_DV))
            fwd = _ret_fwd(xs, cs, rope_tabs, m_lat, m_ctx, g1, w_ri, j, le[0], tm=tm)
            xs, cs = _ret_bwd(xs, fwd[:4], cs, fwd[4:], m_lat, m_ctx, g1, w_ri, w_ro, j, le[1],
                              tm=tm)
        if last:
            xs = _ffn(xs, m_lat, g2, f_in, f_out, i, fg, final=True, tm=tm)
        else:
            xs, cs = _ffn(xs, m_lat, g2, f_in, f_out, i, fg, final=False, tm=tm,
                          ctx=(cs, m_ctx))

    return xs[None]
```
